```python
import math
import jax, jax.numpy as jnp
from jax import lax
import numpy as np

D_MODEL = 2048
BATCH = 4
SEQ = 2048
DEPTH = 4

N_META = 16
POOL_WINDOWS = (2, 4, 8, 16)
POOL_GROUP = 256
D_POOL = POOL_GROUP * 4
GDN_HEADS = 8
GDN_DK = 128
GDN_DV = 128
GDN_QK = GDN_HEADS * GDN_DK
GDN_V = GDN_HEADS * GDN_DV
GDN_CONV = 4
GDN_CHUNK = 64
SB_HEADS = 8
SB_DH = 128
SB_W = SB_HEADS * SB_DH
SB_BLOCK = 128
N_BRANCH = 3
D_BRANCH = 1024
IN_SIZES = (D_POOL, GDN_QK, GDN_QK, GDN_V, GDN_V, GDN_HEADS, GDN_HEADS, SB_W, SB_W, SB_W)
N_IN = D_POOL + 2 * GDN_QK + 2 * GDN_V + 2 * GDN_HEADS + 3 * SB_W
N_EXPERTS = 64
TOP_K = 8
D_EXPERT = 384
D_SHARED = 384
ROUTED_SCALE = 2.5
MOE_BLOCK = 128
DEEPNORM_ALPHA = (2 * DEPTH) ** 0.25
DEEPNORM_BETA = (8 * DEPTH) ** -0.25
LN_EPS = 1e-5
RMS_EPS = 1e-6

kernel_name = 'hybrid_pool_gdn_stickbreak_moe_deepnorm'

F32 = jnp.float32


def _split_points(sizes):
    pts, acc = [], 0
    for s in sizes[:-1]:
        acc += s
        pts.append(acc)
    return pts


def layer_norm(x, g, b):
    xf = x.astype(F32)
    mu = jnp.mean(xf, -1, keepdims=True)
    var = jnp.mean(jnp.square(xf - mu), -1, keepdims=True)
    return ((xf - mu) * lax.rsqrt(var + LN_EPS)).astype(x.dtype) * g + b


def rms_norm(x, g):
    xf = x.astype(F32)
    return xf * lax.rsqrt(jnp.mean(xf * xf, -1, keepdims=True) + RMS_EPS) * g.astype(F32)


def l2_normalize(x):
    return x * lax.rsqrt(jnp.sum(x * x, -1, keepdims=True) + RMS_EPS)


def causal_depthwise_conv(u, w):
    K, C = w.shape
    return lax.conv_general_dilated(u, w[:, None, :].astype(u.dtype), window_strides=(1,),
                                    padding=[(K - 1, 0)], dimension_numbers=('NWC', 'WIO', 'NWC'),
                                    feature_group_count=C)


def pool_mixer(u, pool_w, pool_scale):
    B, L, _ = u.shape
    uf = u.astype(F32)
    c = jnp.concatenate([jnp.zeros((B, 1, D_POOL), F32), jnp.cumsum(uf, axis=1)], axis=1)
    t = jnp.arange(L)
    outs = []
    for i, w in enumerate(POOL_WINDOWS):
        sl = slice(i * POOL_GROUP, (i + 1) * POOL_GROUP)
        cg = c[:, :, sl]
        lower = jnp.concatenate([jnp.zeros((B, w - 1, POOL_GROUP), F32), cg[:, :L - w + 1]], axis=1)
        cnt = jnp.minimum(t + 1, w).astype(F32)[None, :, None]
        outs.append(((cg[:, 1:] - lower) / cnt - uf[:, :, sl]).astype(u.dtype))
    pooled = jnp.stack(outs, axis=2)
    mixed = jnp.einsum('blgc,gcd->blgd', pooled, pool_w).reshape(B, L, D_POOL)
    return mixed * pool_scale


def gated_delta_net(q, k, v, z, a, b, conv_w, A_log, dt_bias, norm_g):
    B, L, _ = q.shape
    H, C = GDN_HEADS, GDN_CHUNK
    out_dtype = q.dtype
    qkv = jax.nn.silu(causal_depthwise_conv(jnp.concatenate([q, k, v], -1), conv_w)).astype(F32)
    q, k, v = jnp.split(qkv, [GDN_QK, 2 * GDN_QK], axis=-1)
    q = l2_normalize(q.reshape(B, L, H, GDN_DK)) * (GDN_DK ** -0.5)
    k = l2_normalize(k.reshape(B, L, H, GDN_DK))
    v = v.reshape(B, L, H, GDN_DV)
    beta = jax.nn.sigmoid(b.astype(F32))
    g = -jnp.exp(A_log.astype(F32)) * jax.nn.softplus(a.astype(F32) + dt_bias.astype(F32))
    pad = (-N_META) % C
    Lp = L + pad
    N = Lp // C

    def to_chunks(t):
        t = jnp.pad(t, [(0, 0), (pad, 0)] + [(0, 0)] * (t.ndim - 2))
        t = t.reshape((B, N, C) + t.shape[2:])
        return jnp.moveaxis(jnp.moveaxis(t, 3, 2), 1, 0)

    qc, kc, vc = to_chunks(q), to_chunks(k), to_chunks(v)
    gc, bc = to_chunks(g), to_chunks(beta)
    gcum = jnp.cumsum(gc, axis=-1)
    idx = jnp.arange(C)
    causal = idx[:, None] >= idx[None, :]
    strict = idx[:, None] > idx[None, :]
    decay = jnp.exp(jnp.where(causal, gcum[..., :, None] - gcum[..., None, :], -jnp.inf))
    kbeta = kc * bc[..., None]
    lmat = jnp.where(strict, jnp.einsum('nbhcd,nbhsd->nbhcs', kbeta, kc) * decay, 0.0)
    amat = lmat + jnp.eye(C, dtype=F32)
    rhs = jnp.concatenate([vc * bc[..., None], kbeta * jnp.exp(gcum)[..., None]], axis=-1)
    sol = lax.linalg.triangular_solve(amat, rhs, left_side=True, lower=True, unit_diagonal=True)
    u_c, w_c = jnp.split(sol, [GDN_DV], axis=-1)
    attn_qk = jnp.einsum('nbhcd,nbhsd->nbhcs', qc, kc) * decay
    q_dec = qc * jnp.exp(gcum)[..., None]
    glast = gcum[..., -1]
    k_dec = kc * jnp.exp(glast[..., None] - gcum)[..., None]

    def step(S, inp):
        u_i, w_i, qd_i, a_i, kd_i, gl_i = inp
        v_new = u_i - jnp.einsum('bhcd,bhde->bhce', w_i, S)
        o = jnp.einsum('bhcd,bhde->bhce', qd_i, S) + jnp.einsum('bhcs,bhse->bhce', a_i, v_new)
        S = S * jnp.exp(gl_i)[..., None, None] + jnp.einsum('bhcd,bhce->bhde', kd_i, v_new)
        return S, o

    S0 = jnp.zeros((B, H, GDN_DK, GDN_DV), F32)
    _, o = lax.scan(step, S0, (u_c, w_c, q_dec, attn_qk, k_dec, glast))
    o = jnp.moveaxis(jnp.moveaxis(o, 0, 1), 2, 3).reshape(B, Lp, H, GDN_DV)[:, pad:]
    o = rms_norm(o, norm_g) * jax.nn.silu(z.astype(F32).reshape(B, L, H, GDN_DV))
    return o.reshape(B, L, GDN_V).astype(out_dtype)


def stick_breaking_attention(q, k, v):
    B, L, _ = q.shape
    out_dtype = q.dtype
    q = q.reshape(B, L, SB_HEADS, SB_DH).transpose(0, 2, 1, 3).astype(F32) * (SB_DH ** -0.5)
    k = k.reshape(B, L, SB_HEADS, SB_DH).transpose(0, 2, 1, 3).astype(F32)
    v = v.reshape(B, L, SB_HEADS, SB_DH).transpose(0, 2, 1, 3).astype(F32)
    lead = L % SB_BLOCK
    starts = ([0] if lead else []) + list(range(lead, L, SB_BLOCK))
    ends = starts[1:] + [L]
    outs = []
    for qs, qe in zip(starts, ends):
        zb = jnp.einsum('bhtd,bhsd->bhts', q[:, :, qs:qe], k[:, :, :qe])
        valid = jnp.arange(qe)[None, :] < jnp.arange(qs, qe)[:, None]
        log_om = jnp.where(valid, jax.nn.log_sigmoid(-zb), 0.0)
        between = lax.cumsum(log_om, axis=3, reverse=True) - log_om
        attn = jnp.where(valid, jnp.exp(jax.nn.log_sigmoid(zb) + between), 0.0)
        outs.append(jnp.einsum('bhts,bhsd->bhtd', attn, v[:, :, :qe]))
    o = jnp.concatenate(outs, axis=2)
    return o.transpose(0, 2, 1, 3).reshape(B, L, SB_W).astype(out_dtype)


def hybrid_mixer(h, w_in, pool_w, pool_scale, conv_w, A_log, dt_bias, gdn_norm_g, w_branch, w_gate, b_gate, w_o):
    B, L, _ = h.shape
    proj = h @ w_in
    u_pool, gq, gk, gv, gz, ga, gb, sq, sk, sv = jnp.split(proj, _split_points(IN_SIZES), axis=-1)
    y_a = pool_mixer(u_pool, pool_w, pool_scale)
    y_b = gated_delta_net(gq, gk, gv, gz, ga, gb, conv_w, A_log, dt_bias, gdn_norm_g)
    y_c = stick_breaking_attention(sq, sk, sv)
    branches = jnp.stack([y_a, y_b, y_c], axis=2)
    up = jnp.einsum('blnc,ncd->blnd', branches, w_branch)
    gates = jax.nn.sigmoid(h @ w_gate + b_gate).reshape(B, L, N_BRANCH, D_MODEL)
    merged = jnp.sum(gates * up, axis=2)
    return merged @ w_o


def routed_experts(h2, router_w, router_bias, e_gate, e_up, e_down):
    T, D = h2.shape
    scores = jax.nn.sigmoid((h2 @ router_w).astype(F32))
    _, idx = lax.top_k(scores + router_bias.astype(F32), TOP_K)
    sel = jnp.take_along_axis(scores, idx, axis=-1)
    wts = sel / jnp.sum(sel, -1, keepdims=True) * ROUTED_SCALE
    A = T * TOP_K
    flat_e = idx.reshape(A)
    flat_t = jnp.repeat(jnp.arange(T, dtype=jnp.int32), TOP_K)
    flat_w = wts.reshape(A)
    order = jnp.argsort(flat_e)
    se, st, sw = flat_e[order], flat_t[order], flat_w[order]
    counts = jnp.bincount(flat_e, length=N_EXPERTS)
    padded = (counts + MOE_BLOCK - 1) // MOE_BLOCK * MOE_BLOCK
    pad_end = jnp.cumsum(padded)
    pad_start = pad_end - padded
    start = jnp.cumsum(counts) - counts
    dest = pad_start[se] + jnp.arange(A, dtype=jnp.int32) - start[se]
    n_blocks = -(-A // MOE_BLOCK) + N_EXPERTS
    P = n_blocks * MOE_BLOCK
    row_tok = jnp.full((P,), T, jnp.int32).at[dest].set(st)
    row_w = jnp.zeros((P,), F32).at[dest].set(sw)
    block_e = jnp.minimum(jnp.searchsorted(pad_end, jnp.arange(n_blocks) * MOE_BLOCK, side='right'), N_EXPERTS - 1)
    h_pad = jnp.concatenate([h2, jnp.zeros((1, D), h2.dtype)], axis=0)

    def run_block(args):
        tok, wr, e = args
        xb = h_pad[tok]
        hid = jax.nn.silu(xb @ e_gate[e]) * (xb @ e_up[e])
        return (hid @ e_down[e]) * wr[:, None].astype(xb.dtype)

    y = lax.map(run_block, (row_tok.reshape(n_blocks, MOE_BLOCK), row_w.reshape(n_blocks, MOE_BLOCK), block_e))
    return jax.ops.segment_sum(y.reshape(P, D), row_tok, num_segments=T + 1)[:T]


def moe_ffn(h2, router_w, router_bias, e_gate, e_up, e_down, s_gate, s_up, s_down):
    shared = (jax.nn.silu(h2 @ s_gate) * (h2 @ s_up)) @ s_down
    return shared + routed_experts(h2, router_w, router_bias, e_gate, e_up, e_down).astype(h2.dtype)


def setup_inputs(seed: int = 0) -> dict:
    key = jax.random.key(seed)
    ks = iter(jax.random.split(key, 40))
    D = D_MODEL

    def nrm(shape, scale):
        return jax.random.normal(next(ks), shape, F32) * scale

    conv_ch = 2 * GDN_QK + GDN_V
    dt = jnp.exp(jax.random.uniform(next(ks), (DEPTH, GDN_HEADS), F32, math.log(1e-3), math.log(1e-1)))
    return {
        'x': nrm((BATCH, SEQ, D), 1.0),
        'meta': nrm((N_META, D), 1.0),
        'ln_in_g': 1.0 + nrm((D,), 0.02),
        'ln_in_b': nrm((D,), 0.02),
        'w_in': nrm((DEPTH, D, N_IN), D ** -0.5),
        'pool_w': nrm((DEPTH, len(POOL_WINDOWS), POOL_GROUP, POOL_GROUP), POOL_GROUP ** -0.5),
        'pool_scale': 1.0 + nrm((DEPTH, D_POOL), 0.1),
        'conv_w': nrm((DEPTH, GDN_CONV, conv_ch), GDN_CONV ** -0.5),
        'A_log': jnp.log(jax.random.uniform(next(ks), (DEPTH, GDN_HEADS), F32, 1.0, 16.0)),
        'dt_bias': dt + jnp.log(-jnp.expm1(-dt)),
        'gdn_norm_g': 1.0 + nrm((DEPTH, GDN_DV), 0.02),
        'w_branch': nrm((DEPTH, N_BRANCH, D_BRANCH, D), D_BRANCH ** -0.5),
        'w_gate': nrm((DEPTH, D, N_BRANCH * D), D ** -0.5),
        'b_gate': nrm((DEPTH, N_BRANCH * D), 0.02),
        'w_o': nrm((DEPTH, D, D), D ** -0.5 * DEEPNORM_BETA),
        'ln1_g': 1.0 + nrm((DEPTH, D), 0.02),
        'ln1_b': nrm((DEPTH, D), 0.02),
        'router_w': nrm((DEPTH, D, N_EXPERTS), D ** -0.5),
        'router_bias': nrm((DEPTH, N_EXPERTS), 0.01),
        'exp_w_gate': nrm((DEPTH, N_EXPERTS, D, D_EXPERT), D ** -0.5),
        'exp_w_up': nrm((DEPTH, N_EXPERTS, D, D_EXPERT), D ** -0.5),
        'exp_w_down': nrm((DEPTH, N_EXPERTS, D_EXPERT, D), D_EXPERT ** -0.5 * DEEPNORM_BETA),
        'sh_w_gate': nrm((DEPTH, D, D_SHARED), D ** -0.5),
        'sh_w_up': nrm((DEPTH, D, D_SHARED), D ** -0.5),
        'sh_w_down': nrm((DEPTH, D_SHARED, D), D_SHARED ** -0.5 * DEEPNORM_BETA),
        'ln2_g': 1.0 + nrm((DEPTH, D), 0.02),
        'ln2_b': nrm((DEPTH, D), 0.02),
    }


def reference(x, meta, ln_in_g, ln_in_b, w_in, pool_w, pool_scale, conv_w, A_log, dt_bias, gdn_norm_g,
              w_branch, w_gate, b_gate, w_o, ln1_g, ln1_b, router_w, router_bias, exp_w_gate, exp_w_up,
              exp_w_down, sh_w_gate, sh_w_up, sh_w_down, ln2_g, ln2_b):
    B = x.shape[0]
    h = jnp.concatenate([jnp.broadcast_to(meta[None].astype(x.dtype), (B, N_META, D_MODEL)), x], axis=1)
    h = layer_norm(h, ln_in_g, ln_in_b)
    L = h.shape[1]
    for l in range(DEPTH):
        mix = hybrid_mixer(h, w_in[l], pool_w[l], pool_scale[l], conv_w[l], A_log[l], dt_bias[l], gdn_norm_g[l],
                           w_branch[l], w_gate[l], b_gate[l], w_o[l])
        h = layer_norm(DEEPNORM_ALPHA * h + mix, ln1_g[l], ln1_b[l])
        ffn = moe_ffn(h.reshape(B * L, D_MODEL), router_w[l], router_bias[l], exp_w_gate[l], exp_w_up[l],
                      exp_w_down[l], sh_w_gate[l], sh_w_up[l], sh_w_down[l])
        h = layer_norm(DEEPNORM_ALPHA * h + ffn.reshape(B, L, D_MODEL), ln2_g[l], ln2_b[l])
    return h[:, N_META:]
```

```python
import functools
import math

import jax
import jax.numpy as jnp
from jax import lax
from jax.experimental import pallas as pl
from jax.experimental.pallas import tpu as pltpu

F32 = jnp.float32
BF16 = jnp.bfloat16
I32 = jnp.int32

D_MODEL = 2048
DEPTH = 4
N_META = 16
ROW_BLOCK = 128
PAD_ROWS = ROW_BLOCK - N_META
POOL_GROUP = 256
N_POOL_GROUPS = 4
D_POOL = POOL_GROUP * N_POOL_GROUPS
HEADS = 8
HEAD_DIM = 128
D_HEADS = HEADS * HEAD_DIM
GDN_CONV = 4
GDN_CHUNK = 64
N_BRANCH = 3
N_EXPERTS = 64
TOP_K = 8
D_EXPERT = 384
ROUTED_SCALE = 2.5
DEEPNORM_ALPHA = (2 * DEPTH) ** 0.25
LN_EPS = 1e-5
RMS_EPS = 1e-6
LANES = 128
N_MAIN = D_POOL + 4 * D_HEADS
N_GATES = 2 * HEADS
N_SB = 3 * D_HEADS
N_SIDE = N_SB + LANES
NEG_INF = float("-inf")


def _cparams(semantics, vmem_mib=None):
    kw = {}
    if vmem_mib is not None:
        kw["vmem_limit_bytes"] = vmem_mib * 2**20
    return pltpu.CompilerParams(dimension_semantics=semantics, **kw)


def _bdot(a, b):
    return jnp.dot(a.astype(BF16), b.astype(BF16), preferred_element_type=F32)


def _bdot_nt(a, b):
    return lax.dot_general(a.astype(BF16), b.astype(BF16), (((1,), (1,)), ((), ())),
                           preferred_element_type=F32)


def _split_bf16(x, parts):
    out, r = [], x
    for _ in range(parts):
        p = r.astype(BF16)
        out.append(p)
        r = r - p.astype(F32)
    return out


def _ln_rows(x, g, b):
    mu = jnp.mean(x, -1, keepdims=True)
    xc = x - mu
    var = jnp.mean(xc * xc, -1, keepdims=True)
    return xc * lax.rsqrt(var + LN_EPS) * g + b


def _silu(x):
    return x * jax.nn.sigmoid(x)


def _softplus(x):
    return jnp.maximum(x, 0.0) + jnp.log1p(jnp.exp(-jnp.abs(x)))


def _ln_in_kernel(x_ref, g_ref, b_ref, h_ref, h16_ref):
    y = _ln_rows(x_ref[...], g_ref[...], b_ref[...])
    h_ref[...] = y
    h16_ref[...] = y.astype(BF16)


def _ln_in(xp, g, b, tm=256):
    tp, d = xp.shape
    return pl.pallas_call(
        _ln_in_kernel,
        grid=(tp // tm,),
        in_specs=[pl.BlockSpec((tm, d), lambda i: (i, 0)),
                  pl.BlockSpec((1, d), lambda i: (0, 0)),
                  pl.BlockSpec((1, d), lambda i: (0, 0))],
        out_specs=[pl.BlockSpec((tm, d), lambda i: (i, 0)),
                   pl.BlockSpec((tm, d), lambda i: (i, 0))],
        out_shape=[jax.ShapeDtypeStruct((tp, d), F32), jax.ShapeDtypeStruct((tp, d), BF16)],
        compiler_params=_cparams(("parallel",)),
        name="ln_in",
    )(xp, g.reshape(1, d), b.reshape(1, d))


def _mm_kernel(x_ref, w_ref, o_ref, wb_ref):
    @pl.when(pl.program_id(1) == 0)
    def _():
        wb_ref[...] = w_ref[...].astype(BF16)

    o_ref[...] = jnp.dot(x_ref[...], wb_ref[...], preferred_element_type=F32).astype(o_ref.dtype)


def _matmul(x16, w, w_spec, n_out, tn, tm=512, name="mm"):
    m, k = x16.shape
    return pl.pallas_call(
        _mm_kernel,
        grid=(n_out // tn, m // tm),
        in_specs=[pl.BlockSpec((tm, k), lambda j, i: (i, 0)), w_spec],
        out_specs=pl.BlockSpec((tm, tn), lambda j, i: (i, j)),
        out_shape=jax.ShapeDtypeStruct((m, n_out), F32),
        scratch_shapes=[pltpu.VMEM((k, tn), BF16)],
        compiler_params=_cparams(("arbitrary", "arbitrary"), 48),
        name=name,
    )(x16, w)


def _pool_kernel(u_ref, w_ref, s_ref, o_ref):
    g = pl.program_id(1)
    lp = u_ref.shape[0]
    t = lax.broadcasted_iota(I32, (lp, 1), 0) - PAD_ROWS
    u = jnp.where(t >= 0, u_ref[...], 0.0)
    s2 = u + pltpu.roll(u, 1, 0)
    s4 = s2 + pltpu.roll(s2, 2, 0)
    s8 = s4 + pltpu.roll(s4, 4, 0)
    s16 = s8 + pltpu.roll(s8, 8, 0)
    s = jnp.where(g == 0, s2, jnp.where(g == 1, s4, jnp.where(g == 2, s8, s16)))
    window = jnp.left_shift(2, g)
    cnt = jnp.maximum(jnp.minimum(t + 1, window), 1).astype(F32)
    pooled = s / cnt - u
    y = _bdot(pooled, w_ref[...]) * s_ref[...]
    o_ref[...] = y.astype(o_ref.dtype)


def _pool_mixer(proj_main, pool_w, pool_scale, batch, lp):
    tp = proj_main.shape[0]
    return pl.pallas_call(
        _pool_kernel,
        grid=(batch, N_POOL_GROUPS),
        in_specs=[pl.BlockSpec((lp, POOL_GROUP), lambda b, g: (b, g)),
                  pl.BlockSpec((None, POOL_GROUP, POOL_GROUP), lambda b, g: (g, 0, 0)),
                  pl.BlockSpec((1, POOL_GROUP), lambda b, g: (0, g))],
        out_specs=pl.BlockSpec((lp, POOL_GROUP), lambda b, g: (b, g)),
        out_shape=jax.ShapeDtypeStruct((tp, D_POOL), BF16),
        compiler_params=_cparams(("parallel", "parallel"), 48),
        name="pool_mixer",
    )(proj_main, pool_w, pool_scale.reshape(1, D_POOL))


def _gdn_conv_kernel(u_ref, w_ref, o_ref):
    c = pl.program_id(1)
    lp = u_ref.shape[0]
    t = lax.broadcasted_iota(I32, (lp, 1), 0) - PAD_ROWS
    u = jnp.where(t >= 0, u_ref[...], 0.0)
    w = w_ref[...]
    y = (u * w[3:4] + pltpu.roll(u, 1, 0) * w[2:3] + pltpu.roll(u, 2, 0) * w[1:2]
         + pltpu.roll(u, 3, 0) * w[0:1])
    y = _silu(y)
    nrm = y * lax.rsqrt(jnp.sum(y * y, -1, keepdims=True) + RMS_EPS)
    o_ref[...] = jnp.where(c < HEADS, nrm * (HEAD_DIM ** -0.5), jnp.where(c < 2 * HEADS, nrm, y))


def _gdn_conv(proj_main, conv_w, batch, lp):
    tp = proj_main.shape[0]
    first = D_POOL // HEAD_DIM
    return pl.pallas_call(
        _gdn_conv_kernel,
        grid=(batch, 3 * HEADS),
        in_specs=[pl.BlockSpec((lp, HEAD_DIM), lambda b, c: (b, first + c)),
                  pl.BlockSpec((GDN_CONV, HEAD_DIM), lambda b, c: (0, c))],
        out_specs=pl.BlockSpec((lp, HEAD_DIM), lambda b, c: (b, c)),
        out_shape=jax.ShapeDtypeStruct((tp, 3 * D_HEADS), F32),
        compiler_params=_cparams(("parallel", "parallel"), 48),
        name="gdn_conv",
    )(proj_main, conv_w)


def _gdn_chunk_kernel(q_ref, k_ref, v_ref, z_ref, ab_ref, alog_ref, dtb_ref, ng_ref, o_ref, s_ref):
    c = pl.program_id(1)
    C = GDN_CHUNK

    @pl.when(c == 0)
    def _():
        s_ref[...] = jnp.zeros_like(s_ref)

    ab = ab_ref[...]
    pos = c * C + lax.broadcasted_iota(I32, (C, 1), 0)
    real = pos >= PAD_ROWS
    g_all = jnp.where(real, -jnp.exp(alog_ref[...]) * _softplus(ab + dtb_ref[...]), 0.0)
    beta_all = jnp.where(real, jax.nn.sigmoid(ab), 0.0)
    ri = lax.broadcasted_iota(I32, (C, C), 0)
    ci = lax.broadcasted_iota(I32, (C, C), 1)
    causal = ri >= ci
    strict = ri > ci
    tri = causal.astype(BF16)
    eye = (ri == ci).astype(F32)
    gcum = sum(jnp.dot(tri, p, preferred_element_type=F32) for p in _split_bf16(g_all, 3))
    gcum_t = jnp.concatenate([gcum, jnp.zeros_like(gcum)], axis=0).T
    ng = ng_ref[...]

    for h in range(HEADS):
        sl = slice(h * HEAD_DIM, (h + 1) * HEAD_DIM)
        gc_col = gcum[:, h:h + 1]
        gc_row = gcum_t[h:h + 1, 0:C]
        g_last = gcum[C - 1:C, h:h + 1]
        beta = beta_all[:, HEADS + h:HEADS + h + 1]
        decay = jnp.where(causal, jnp.exp(jnp.where(causal, gc_col - gc_row, 0.0)), 0.0)
        qh = q_ref[:, sl]
        kh = k_ref[:, sl]
        vh = v_ref[:, sl]
        eg = jnp.exp(gc_col)
        kbeta = kh * beta
        kk = _bdot_nt(kbeta, kh)
        lmat = jnp.where(strict, kk * decay, 0.0)
        tinv = eye - lmat
        pw = _bdot(lmat, lmat)
        for it in range(5):
            tinv = tinv + _bdot(tinv, pw)
            if it < 4:
                pw = _bdot(pw, pw)
        rhs = jnp.concatenate([vh * beta, kbeta * eg], axis=-1)
        sol = _bdot(tinv, rhs)
        u_c = sol[:, :HEAD_DIM]
        w_c = sol[:, HEAD_DIM:]
        attn = _bdot_nt(qh, kh) * decay
        q_dec = qh * eg
        k_dec = kh * jnp.exp(g_last - gc_col)
        state = s_ref[h]
        v_new = u_c - _bdot(w_c, state)
        o = _bdot(q_dec, state) + _bdot(attn, v_new)
        s_ref[h] = state * jnp.exp(g_last) + _bdot(k_dec.T, v_new)
        o = o * lax.rsqrt(jnp.mean(o * o, -1, keepdims=True) + RMS_EPS) * ng
        o_ref[:, sl] = (o * _silu(z_ref[:, sl])).astype(o_ref.dtype)


def _gdn(qkv, proj_main, proj_side, a_log, dt_bias, norm_g, batch, lp):
    tp = qkv.shape[0]
    nc = lp // GDN_CHUNK
    zcol = (D_POOL + 3 * D_HEADS) // D_HEADS
    alog_pad = jnp.zeros((1, LANES), F32).at[0, :HEADS].set(a_log)
    dtb_pad = jnp.zeros((1, LANES), F32).at[0, :HEADS].set(dt_bias)
    row = lambda b, c: b * nc + c
    return pl.pallas_call(
        _gdn_chunk_kernel,
        grid=(batch, nc),
        in_specs=[pl.BlockSpec((GDN_CHUNK, D_HEADS), lambda b, c: (row(b, c), 0)),
                  pl.BlockSpec((GDN_CHUNK, D_HEADS), lambda b, c: (row(b, c), 1)),
                  pl.BlockSpec((GDN_CHUNK, D_HEADS), lambda b, c: (row(b, c), 2)),
                  pl.BlockSpec((GDN_CHUNK, D_HEADS), lambda b, c: (row(b, c), zcol)),
                  pl.BlockSpec((GDN_CHUNK, LANES), lambda b, c: (row(b, c), N_SB // LANES)),
                  pl.BlockSpec((1, LANES), lambda b, c: (0, 0)),
                  pl.BlockSpec((1, LANES), lambda b, c: (0, 0)),
                  pl.BlockSpec((1, HEAD_DIM), lambda b, c: (0, 0))],
        out_specs=pl.BlockSpec((GDN_CHUNK, D_HEADS), lambda b, c: (row(b, c), 0)),
        out_shape=jax.ShapeDtypeStruct((tp, D_HEADS), BF16),
        scratch_shapes=[pltpu.VMEM((HEADS, HEAD_DIM, HEAD_DIM), F32)],
        compiler_params=_cparams(("parallel", "arbitrary"), 48),
        name="gdn_chunks",
    )(qkv, qkv, qkv, proj_main, proj_side, alog_pad, dtb_pad, norm_g.reshape(1, HEAD_DIM))


def _sb_kernel(q_ref, k_ref, v_ref, o_ref):
    i = pl.program_id(2)
    T = ROW_BLOCK
    q = (q_ref[...] * (HEAD_DIM ** -0.5)).astype(BF16)
    qpos = i * T + lax.broadcasted_iota(I32, (T, T), 0)
    kloc = lax.broadcasted_iota(I32, (T, T), 1)
    suffix = (lax.broadcasted_iota(I32, (T, T), 0) > kloc).astype(BF16)

    def body(n, carry):
        later, acc = carry
        j = i - n
        start = pl.multiple_of(j * T, T)
        kj = k_ref[pl.ds(start, T), :]
        vj = v_ref[pl.ds(start, T), :]
        z = _bdot_nt(q, kj)
        kpos = start + kloc
        valid = (kpos < qpos) & (kpos >= PAD_ROWS)
        sp = _softplus(z)
        lom = jnp.where(valid, -sp, 0.0)
        between = later + sum(jnp.dot(p, suffix, preferred_element_type=F32)
                              for p in _split_bf16(lom, 2))
        attn = jnp.where(valid, jnp.exp(z - sp + between), 0.0)
        acc = acc + _bdot(attn, vj)
        later = later + jnp.sum(lom, -1, keepdims=True)
        return later, acc

    init = (jnp.zeros((T, 1), F32), jnp.zeros((T, HEAD_DIM), F32))
    _, acc = lax.fori_loop(0, i + 1, body, init)
    o_ref[...] = acc.astype(o_ref.dtype)


def _stick_breaking(proj_side, batch, lp):
    tp = proj_side.shape[0]
    nq = lp // ROW_BLOCK
    return pl.pallas_call(
        _sb_kernel,
        grid=(batch, HEADS, nq),
        in_specs=[pl.BlockSpec((ROW_BLOCK, HEAD_DIM), lambda b, h, i: (b * nq + i, h)),
                  pl.BlockSpec((lp, HEAD_DIM), lambda b, h, i: (b, HEADS + h)),
                  pl.BlockSpec((lp, HEAD_DIM), lambda b, h, i: (b, 2 * HEADS + h))],
        out_specs=pl.BlockSpec((ROW_BLOCK, HEAD_DIM), lambda b, h, i: (b * nq + i, h)),
        out_shape=jax.ShapeDtypeStruct((tp, D_HEADS), BF16),
        compiler_params=_cparams(("parallel", "parallel", "arbitrary"), 48),
        name="stick_breaking",
    )(proj_side, proj_side, proj_side)


def _merge_kernel(h_ref, ya_ref, yb_ref, yc_ref, wg0_ref, wg1_ref, wg2_ref, bg0_ref, bg1_ref, bg2_ref,
                  wb_ref, o_ref, wg_s, wb_s):
    @pl.when(pl.program_id(1) == 0)
    def _():
        wg_s[0] = wg0_ref[...].astype(BF16)
        wg_s[1] = wg1_ref[...].astype(BF16)
        wg_s[2] = wg2_ref[...].astype(BF16)
        wb_s[...] = wb_ref[...].astype(BF16)

    h = h_ref[...]
    acc = None
    for n, (y_ref, bg_ref) in enumerate(((ya_ref, bg0_ref), (yb_ref, bg1_ref), (yc_ref, bg2_ref))):
        gate = jax.nn.sigmoid(jnp.dot(h, wg_s[n], preferred_element_type=F32) + bg_ref[...])
        up = jnp.dot(y_ref[...], wb_s[n], preferred_element_type=F32)
        acc = gate * up if acc is None else acc + gate * up
    o_ref[...] = acc.astype(o_ref.dtype)


def _merge(h16, ya, yb, yc, w_gate, b_gate, w_branch, l, tn=256, tm=512):
    tp = h16.shape[0]
    nj = D_MODEL // tn
    wg_spec = lambda n: pl.BlockSpec((None, D_MODEL, tn), lambda j, i: (l, 0, n * nj + j))
    bg_spec = lambda n: pl.BlockSpec((None, 1, tn), lambda j, i: (l, 0, n * nj + j))
    y_spec = pl.BlockSpec((tm, D_HEADS), lambda j, i: (i, 0))
    return pl.pallas_call(
        _merge_kernel,
        grid=(nj, tp // tm),
        in_specs=[pl.BlockSpec((tm, D_MODEL), lambda j, i: (i, 0)), y_spec, y_spec, y_spec,
                  wg_spec(0), wg_spec(1), wg_spec(2), bg_spec(0), bg_spec(1), bg_spec(2),
                  pl.BlockSpec((None, N_BRANCH, D_HEADS, tn), lambda j, i: (l, 0, 0, j))],
        out_specs=pl.BlockSpec((tm, tn), lambda j, i: (i, j)),
        out_shape=jax.ShapeDtypeStruct((tp, D_MODEL), BF16),
        scratch_shapes=[pltpu.VMEM((N_BRANCH, D_MODEL, tn), BF16),
                        pltpu.VMEM((N_BRANCH, D_HEADS, tn), BF16)],
        compiler_params=_cparams(("arbitrary", "arbitrary"), 56),
        name="branch_merge",
    )(h16, ya, yb, yc, w_gate, w_gate, w_gate, b_gate, b_gate, b_gate, w_branch)


def _oproj_ln_kernel(m_ref, w_ref, h_ref, g_ref, b_ref, ho_ref, h16o_ref):
    mix = jnp.dot(m_ref[...], w_ref[...], preferred_element_type=F32)
    y = _ln_rows(DEEPNORM_ALPHA * h_ref[...] + mix, g_ref[...], b_ref[...])
    ho_ref[...] = y
    h16o_ref[...] = y.astype(BF16)


def _oproj_ln(merged, w_o16, h, g, b, tm=256):
    tp, d = h.shape
    row = pl.BlockSpec((tm, d), lambda i: (i, 0))
    vec = pl.BlockSpec((1, d), lambda i: (0, 0))
    return pl.pallas_call(
        _oproj_ln_kernel,
        grid=(tp // tm,),
        in_specs=[row, pl.BlockSpec((d, d), lambda i: (0, 0)), row, vec, vec],
        out_specs=[row, row],
        out_shape=[jax.ShapeDtypeStruct((tp, d), F32), jax.ShapeDtypeStruct((tp, d), BF16)],
        compiler_params=_cparams(("parallel",), 48),
        name="oproj_ln",
    )(merged, w_o16, h, g.reshape(1, d), b.reshape(1, d))


def _router_kernel(h_ref, w_ref, b_ref, idx_ref, wts_ref, rank_ref, cnt_ref, base_s, *, blocks_per_seq):
    i = pl.program_id(0)
    tm = h_ref.shape[0]

    @pl.when(i == 0)
    def _():
        base_s[...] = jnp.zeros_like(base_s)

    xh, xl = _split_bf16(h_ref[...], 2)
    wh, wl = _split_bf16(w_ref[...], 2)
    dot = functools.partial(jnp.dot, preferred_element_type=F32)
    scores = jax.nn.sigmoid(dot(xh, wh) + dot(xh, wl) + dot(xl, wh))
    lane = lax.broadcasted_iota(I32, (tm, LANES), 1).astype(F32)
    sel = jnp.where(lane < N_EXPERTS, scores + b_ref[...], NEG_INF)
    row = lax.broadcasted_iota(I32, (tm, 1), 0)
    real = jnp.logical_not((i % blocks_per_seq == 0) & (row < PAD_ROWS))
    chosen = jnp.zeros((tm, LANES), F32)
    idx_out = jnp.zeros((tm, LANES), F32)
    sc_out = jnp.zeros((tm, LANES), F32)
    picks = []
    for k in range(TOP_K):
        m = jnp.max(sel, -1, keepdims=True)
        ik = jnp.min(jnp.where(sel == m, lane, float(LANES)), -1, keepdims=True)
        hit = lane == ik
        sk = jnp.sum(jnp.where(hit, scores, 0.0), -1, keepdims=True)
        sel = jnp.where(hit, NEG_INF, sel)
        chosen = jnp.where(hit, 1.0, chosen)
        idx_out = jnp.where(lane == k, ik, idx_out)
        sc_out = jnp.where(lane == k, sk, sc_out)
        picks.append(hit)
    total = jnp.sum(sc_out, -1, keepdims=True)
    wts_ref[...] = jnp.where(real, sc_out / total * ROUTED_SCALE, 0.0)
    idx_ref[...] = idx_out.astype(I32)
    chosen = jnp.where(real, chosen, 0.0)
    before = (lax.broadcasted_iota(I32, (tm, tm), 0) > lax.broadcasted_iota(I32, (tm, tm), 1)).astype(BF16)
    count_before = dot(before, chosen.astype(BF16)) + base_s[0:1, :]
    rank_out = jnp.zeros((tm, LANES), F32)
    for k in range(TOP_K):
        rk = jnp.sum(jnp.where(picks[k], count_before, 0.0), -1, keepdims=True)
        rank_out = jnp.where(lane == k, rk, rank_out)
    rank_ref[...] = rank_out.astype(I32)
    base_s[...] = base_s[...] + jnp.sum(chosen, 0, keepdims=True)
    cnt_ref[...] = base_s[...].astype(I32)


def _router(h, router_w, router_bias, lp, tm=ROW_BLOCK):
    tp, d = h.shape
    w_pad = jnp.zeros((d, LANES), F32).at[:, :N_EXPERTS].set(router_w)
    b_pad = jnp.zeros((1, LANES), F32).at[0, :N_EXPERTS].set(router_bias)
    tok = pl.BlockSpec((tm, LANES), lambda i: (i, 0))
    return pl.pallas_call(
        functools.partial(_router_kernel, blocks_per_seq=lp // tm),
        grid=(tp // tm,),
        in_specs=[pl.BlockSpec((tm, d), lambda i: (i, 0)),
                  pl.BlockSpec((d, LANES), lambda i: (0, 0)),
                  pl.BlockSpec((1, LANES), lambda i: (0, 0))],
        out_specs=[tok, tok, tok, pl.BlockSpec((8, LANES), lambda i: (0, 0))],
        out_shape=[jax.ShapeDtypeStruct((tp, LANES), I32), jax.ShapeDtypeStruct((tp, LANES), F32),
                   jax.ShapeDtypeStruct((tp, LANES), I32), jax.ShapeDtypeStruct((8, LANES), I32)],
        scratch_shapes=[pltpu.VMEM((8, LANES), F32)],
        compiler_params=_cparams(("arbitrary",)),
        name="moe_router",
    )(h, w_pad, b_pad)


def _row_gather_start(idx_ref, n_rows, src_hbm, dst, sem):
    def body(r, carry):
        t = idx_ref[0, 0, r]
        pltpu.make_async_copy(src_hbm.at[pl.ds(t, 1), :], dst.at[pl.ds(r, 1), :], sem).start()
        return carry

    lax.fori_loop(0, n_rows, body, 0)


def _row_gather_wait(n_rows, src_hbm, dst, sem):
    pltpu.make_async_copy(src_hbm.at[pl.ds(0, n_rows), :], dst, sem).wait()


def _ffn_kernel(be_ref, nu_ref, tokc_ref, tokn_ref, h_hbm, wg_ref, wu_ref, wd_ref, y_ref,
                xbuf, sem, wg_s, wu_s, wd_s):
    i = pl.program_id(0)
    n_used = nu_ref[0]
    slot = i % 2

    @pl.when(i == 0)
    def _():
        _row_gather_start(tokc_ref, ROW_BLOCK, h_hbm, xbuf.at[0], sem.at[0])

    @pl.when(i < n_used)
    def _():
        _row_gather_wait(ROW_BLOCK, h_hbm, xbuf.at[slot], sem.at[slot])

    @pl.when(i + 1 < n_used)
    def _():
        _row_gather_start(tokn_ref, ROW_BLOCK, h_hbm, xbuf.at[1 - slot], sem.at[1 - slot])

    prev = be_ref[jnp.maximum(i - 1, 0)]

    @pl.when((i == 0) | (be_ref[i] != prev))
    def _():
        wg_s[...] = wg_ref[...].astype(BF16)
        wu_s[...] = wu_ref[...].astype(BF16)
        wd_s[...] = wd_ref[...].astype(BF16)

    @pl.when(i < n_used)
    def _():
        x = xbuf[slot].astype(BF16)
        hid = _silu(jnp.dot(x, wg_s[...], preferred_element_type=F32)) * jnp.dot(
            x, wu_s[...], preferred_element_type=F32)
        y_ref[...] = jnp.dot(hid.astype(BF16), wd_s[...], preferred_element_type=F32)

    @pl.when(i >= n_used)
    def _():
        y_ref[...] = jnp.zeros_like(y_ref)


def _expert_ffn(h, row_tok, block_e, n_used, e_gate, e_up, e_down, l):
    tp, d = h.shape
    nb = block_e.shape[0]
    tok3 = row_tok.reshape(nb, 1, ROW_BLOCK)
    smem_blk = lambda f: pl.BlockSpec((1, 1, ROW_BLOCK), f, memory_space=pltpu.SMEM)
    grid_spec = pltpu.PrefetchScalarGridSpec(
        num_scalar_prefetch=2,
        grid=(nb,),
        in_specs=[smem_blk(lambda i, be, nu: (i, 0, 0)),
                  smem_blk(lambda i, be, nu: (jnp.minimum(i + 1, nb - 1), 0, 0)),
                  pl.BlockSpec(memory_space=pl.ANY),
                  pl.BlockSpec((None, None, d, D_EXPERT), lambda i, be, nu: (l, be[i], 0, 0)),
                  pl.BlockSpec((None, None, d, D_EXPERT), lambda i, be, nu: (l, be[i], 0, 0)),
                  pl.BlockSpec((None, None, D_EXPERT, d), lambda i, be, nu: (l, be[i], 0, 0))],
        out_specs=pl.BlockSpec((ROW_BLOCK, d), lambda i, be, nu: (i, 0)),
        scratch_shapes=[pltpu.VMEM((2, ROW_BLOCK, d), F32),
                        pltpu.SemaphoreType.DMA((2,)),
                        pltpu.VMEM((d, D_EXPERT), BF16),
                        pltpu.VMEM((d, D_EXPERT), BF16),
                        pltpu.VMEM((D_EXPERT, d), BF16)],
    )
    return pl.pallas_call(
        _ffn_kernel,
        grid_spec=grid_spec,
        out_shape=jax.ShapeDtypeStruct((nb * ROW_BLOCK, d), F32),
        compiler_params=_cparams(("arbitrary",), 48),
        name="expert_ffn",
    )(block_e, n_used, tok3, tok3, h, e_gate, e_up, e_down)


def _combine_kernel(dc_ref, dn_ref, y_hbm, h_ref, h16_ref, wts_ref, sg_ref, su_ref, sd_ref, g_ref, b_ref,
                    ho_ref, h16o_ref, ybuf, sem):
    i = pl.program_id(0)
    n = pl.num_programs(0)
    tm = h_ref.shape[0]
    slot = i % 2

    @pl.when(i == 0)
    def _():
        _row_gather_start(dc_ref, TOP_K * tm, y_hbm, ybuf.at[0], sem.at[0])

    _row_gather_wait(TOP_K * tm, y_hbm, ybuf.at[slot], sem.at[slot])

    @pl.when(i + 1 < n)
    def _():
        _row_gather_start(dn_ref, TOP_K * tm, y_hbm, ybuf.at[1 - slot], sem.at[1 - slot])

    x = h16_ref[...]
    hid = _silu(jnp.dot(x, sg_ref[...], preferred_element_type=F32)) * jnp.dot(
        x, su_ref[...], preferred_element_type=F32)
    acc = jnp.dot(hid.astype(BF16), sd_ref[...], preferred_element_type=F32)
    wts = wts_ref[...]
    for k in range(TOP_K):
        acc = acc + ybuf[slot, pl.ds(k * tm, tm), :] * wts[:, k:k + 1]
    y = _ln_rows(DEEPNORM_ALPHA * h_ref[...] + acc, g_ref[...], b_ref[...])
    ho_ref[...] = y
    h16o_ref[...] = y.astype(BF16)


def _combine(y, dest, h, h16, wts, sg16, su16, sd16, g, b, tm=ROW_BLOCK):
    tp, d = h.shape
    nt = tp // tm
    dest3 = dest.reshape(nt, tm, TOP_K).transpose(0, 2, 1).reshape(nt, 1, TOP_K * tm)
    smem_blk = lambda f: pl.BlockSpec((1, 1, TOP_K * tm), f, memory_space=pltpu.SMEM)
    row = pl.BlockSpec((tm, d), lambda i: (i, 0))
    vec = pl.BlockSpec((1, d), lambda i: (0, 0))
    return pl.pallas_call(
        _combine_kernel,
        grid=(nt,),
        in_specs=[smem_blk(lambda i: (i, 0, 0)),
                  smem_blk(lambda i: (jnp.minimum(i + 1, nt - 1), 0, 0)),
                  pl.BlockSpec(memory_space=pl.ANY),
                  row, row,
                  pl.BlockSpec((tm, LANES), lambda i: (i, 0)),
                  pl.BlockSpec((d, D_EXPERT), lambda i: (0, 0)),
                  pl.BlockSpec((d, D_EXPERT), lambda i: (0, 0)),
                  pl.BlockSpec((D_EXPERT, d), lambda i: (0, 0)),
                  vec, vec],
        out_specs=[row, row],
        out_shape=[jax.ShapeDtypeStruct((tp, d), F32), jax.ShapeDtypeStruct((tp, d), BF16)],
        scratch_shapes=[pltpu.VMEM((2, TOP_K * tm, d), F32), pltpu.SemaphoreType.DMA((2,))],
        compiler_params=_cparams(("arbitrary",), 56),
        name="moe_combine",
    )(dest3, dest3, y, h, h16, wts, sg16, su16, sd16, g.reshape(1, d), b.reshape(1, d))


def _routing_tables(idx, rank, counts, lp):
    tp = idx.shape[0]
    n_blocks = -(-(tp * TOP_K) // ROW_BLOCK) + N_EXPERTS
    n_rows = n_blocks * ROW_BLOCK
    padded = (counts + ROW_BLOCK - 1) // ROW_BLOCK * ROW_BLOCK
    pad_end = jnp.cumsum(padded)
    pad_start = pad_end - padded
    real = (jnp.arange(tp, dtype=I32) % lp) >= PAD_ROWS
    dest = pad_start[idx] + rank
    scatter_to = jnp.where(real[:, None], dest, n_rows).reshape(-1)
    tok = jnp.repeat(jnp.arange(tp, dtype=I32), TOP_K)
    row_tok = jnp.zeros((n_rows,), I32).at[scatter_to].set(tok, mode="drop")
    block_e = jnp.minimum(jnp.searchsorted(pad_end, jnp.arange(n_blocks, dtype=I32) * ROW_BLOCK, side="right"),
                          N_EXPERTS - 1).astype(I32)
    n_used = (pad_end[-1] // ROW_BLOCK).astype(I32).reshape(1)
    return row_tok, block_e, n_used, jnp.where(real[:, None], dest, 0)


def kernel(x, meta, ln_in_g, ln_in_b, w_in, pool_w, pool_scale, conv_w, A_log, dt_bias, gdn_norm_g, w_branch, w_gate, b_gate, w_o, ln1_g, ln1_b, router_w, router_bias, exp_w_gate, exp_w_up, exp_w_down, sh_w_gate, sh_w_up, sh_w_down, ln2_g, ln2_b):
    batch, seq, d = x.shape
    depth = w_in.shape[0]
    lp = PAD_ROWS + N_META + seq
    assert seq % ROW_BLOCK == 0 and d == D_MODEL
    tp = batch * lp
    xp = jnp.concatenate([jnp.zeros((batch, PAD_ROWS, d), x.dtype),
                          jnp.broadcast_to(meta[None].astype(x.dtype), (batch, N_META, d)), x], axis=1)
    h, h16 = _ln_in(xp.reshape(tp, d), ln_in_g, ln_in_b)
    b_gate3 = b_gate.reshape(depth, 1, N_BRANCH * d)
    tm = 512 if tp % 512 == 0 else ROW_BLOCK

    for l in range(depth):
        w_l = w_in[l]
        w_side = jnp.concatenate([w_l[:, N_MAIN + N_GATES:], w_l[:, N_MAIN:N_MAIN + N_GATES],
                                  jnp.zeros((d, LANES - N_GATES), F32)], axis=1)
        proj_main = _matmul(h16, w_in, pl.BlockSpec((None, d, 512), lambda j, i: (l, 0, j)),
                            N_MAIN, 512, tm, name="proj_main")
        proj_side = _matmul(h16, w_side, pl.BlockSpec((d, 640), lambda j, i: (0, j)),
                            N_SIDE, 640, tm, name="proj_side")
        y_a = _pool_mixer(proj_main, pool_w[l], pool_scale[l], batch, lp)
        qkv = _gdn_conv(proj_main, conv_w[l], batch, lp)
        y_b = _gdn(qkv, proj_main, proj_side, A_log[l], dt_bias[l], gdn_norm_g[l], batch, lp)
        y_c = _stick_breaking(proj_side, batch, lp)
        merged = _merge(h16, y_a, y_b, y_c, w_gate, b_gate3, w_branch, l, tm=tm)
        h, h16 = _oproj_ln(merged, w_o[l].astype(BF16), h, ln1_g[l], ln1_b[l])

        idx, wts, rank, cnt = _router(h, router_w[l], router_bias[l], lp)
        row_tok, block_e, n_used, dest = _routing_tables(idx[:, :TOP_K], rank[:, :TOP_K], cnt[0, :N_EXPERTS], lp)
        y = _expert_ffn(h, row_tok, block_e, n_used, exp_w_gate, exp_w_up, exp_w_down, l)
        h, h16 = _combine(y, dest, h, h16, wts, sh_w_gate[l].astype(BF16), sh_w_up[l].astype(BF16),
                          sh_w_down[l].astype(BF16), ln2_g[l], ln2_b[l])

    return h.reshape(batch, lp, d)[:, PAD_ROWS + N_META:]
```

```python
import functools
import math

import jax
import jax.numpy as jnp
from jax import lax
from jax.experimental import pallas as pl
from jax.experimental.pallas import tpu as pltpu

F32 = jnp.float32
BF16 = jnp.bfloat16
I32 = jnp.int32

D_MODEL = 2048
DEPTH = 4
N_META = 16
ROW_BLOCK = 128
PAD_ROWS = ROW_BLOCK - N_META
POOL_GROUP = 256
N_POOL_GROUPS = 4
D_POOL = POOL_GROUP * N_POOL_GROUPS
HEADS = 8
HEAD_DIM = 128
D_HEADS = HEADS * HEAD_DIM
GDN_CONV = 4
GDN_CHUNK = 64
N_BRANCH = 3
N_EXPERTS = 64
TOP_K = 8
D_EXPERT = 384
ROUTED_SCALE = 2.5
DEEPNORM_ALPHA = (2 * DEPTH) ** 0.25
LN_EPS = 1e-5
RMS_EPS = 1e-6
LANES = 128
N_MAIN = D_POOL + 4 * D_HEADS
N_GATES = 2 * HEADS
N_SB = 3 * D_HEADS
N_SIDE = N_SB + LANES
NEG_INF = float("-inf")


def _cparams(semantics, vmem_mib=None):
    kw = {}
    if vmem_mib is not None:
        kw["vmem_limit_bytes"] = vmem_mib * 2**20
    return pltpu.CompilerParams(dimension_semantics=semantics, **kw)


def _bdot(a, b):
    return jnp.dot(a.astype(BF16), b.astype(BF16), preferred_element_type=F32)


def _bdot_nt(a, b):
    return lax.dot_general(a.astype(BF16), b.astype(BF16), (((1,), (1,)), ((), ())),
                           preferred_element_type=F32)


def _split_bf16(x, parts):
    out, r = [], x
    for _ in range(parts):
        p = r.astype(BF16)
        out.append(p)
        r = r - p.astype(F32)
    return out


def _ln_rows(x, g, b):
    mu = jnp.mean(x, -1, keepdims=True)
    xc = x - mu
    var = jnp.mean(xc * xc, -1, keepdims=True)
    return xc * lax.rsqrt(var + LN_EPS) * g + b


def _silu(x):
    return x * jax.nn.sigmoid(x)


def _softplus(x):
    return jnp.maximum(x, 0.0) + jnp.log1p(jnp.exp(-jnp.abs(x)))


def _ln_in_kernel(x_ref, g_ref, b_ref, h_ref, h16_ref):
    y = _ln_rows(x_ref[...], g_ref[...], b_ref[...])
    h_ref[...] = y
    h16_ref[...] = y.astype(BF16)


def _ln_in(xp, g, b, tm=256):
    tp, d = xp.shape
    return pl.pallas_call(
        _ln_in_kernel,
        grid=(tp // tm,),
        in_specs=[pl.BlockSpec((tm, d), lambda i: (i, 0)),
                  pl.BlockSpec((1, d), lambda i: (0, 0)),
                  pl.BlockSpec((1, d), lambda i: (0, 0))],
        out_specs=[pl.BlockSpec((tm, d), lambda i: (i, 0)),
                   pl.BlockSpec((tm, d), lambda i: (i, 0))],
        out_shape=[jax.ShapeDtypeStruct((tp, d), F32), jax.ShapeDtypeStruct((tp, d), BF16)],
        compiler_params=_cparams(("parallel",)),
        name="ln_in",
    )(xp, g.reshape(1, d), b.reshape(1, d))


def _mm_kernel(x_ref, w_ref, o_ref, wb_ref):
    @pl.when(pl.program_id(1) == 0)
    def _():
        wb_ref[...] = w_ref[...].astype(BF16)

    o_ref[...] = jnp.dot(x_ref[...], wb_ref[...], preferred_element_type=F32).astype(o_ref.dtype)


def _matmul(x16, w, w_spec, n_out, tn, tm=512, name="mm"):
    m, k = x16.shape
    return pl.pallas_call(
        _mm_kernel,
        grid=(n_out // tn, m // tm),
        in_specs=[pl.BlockSpec((tm, k), lambda j, i: (i, 0)), w_spec],
        out_specs=pl.BlockSpec((tm, tn), lambda j, i: (i, j)),
        out_shape=jax.ShapeDtypeStruct((m, n_out), F32),
        scratch_shapes=[pltpu.VMEM((k, tn), BF16)],
        compiler_params=_cparams(("arbitrary", "arbitrary"), 48),
        name=name,
    )(x16, w)


def _pool_kernel(u_ref, w_ref, s_ref, o_ref):
    g = pl.program_id(1)
    lp = u_ref.shape[0]
    t = lax.broadcasted_iota(I32, (lp, 1), 0) - PAD_ROWS
    u = jnp.where(t >= 0, u_ref[...], 0.0)
    s2 = u + pltpu.roll(u, 1, 0)
    s4 = s2 + pltpu.roll(s2, 2, 0)
    s8 = s4 + pltpu.roll(s4, 4, 0)
    s16 = s8 + pltpu.roll(s8, 8, 0)
    s = jnp.where(g == 0, s2, jnp.where(g == 1, s4, jnp.where(g == 2, s8, s16)))
    window = jnp.left_shift(2, g)
    cnt = jnp.maximum(jnp.minimum(t + 1, window), 1).astype(F32)
    pooled = s / cnt - u
    y = _bdot(pooled, w_ref[...]) * s_ref[...]
    o_ref[...] = y.astype(o_ref.dtype)


def _pool_mixer(proj_main, pool_w, pool_scale, batch, lp):
    tp = proj_main.shape[0]
    return pl.pallas_call(
        _pool_kernel,
        grid=(batch, N_POOL_GROUPS),
        in_specs=[pl.BlockSpec((lp, POOL_GROUP), lambda b, g: (b, g)),
                  pl.BlockSpec((None, POOL_GROUP, POOL_GROUP), lambda b, g: (g, 0, 0)),
                  pl.BlockSpec((1, POOL_GROUP), lambda b, g: (0, g))],
        out_specs=pl.BlockSpec((lp, POOL_GROUP), lambda b, g: (b, g)),
        out_shape=jax.ShapeDtypeStruct((tp, D_POOL), BF16),
        compiler_params=_cparams(("parallel", "parallel"), 48),
        name="pool_mixer",
    )(proj_main, pool_w, pool_scale.reshape(1, D_POOL))


def _gdn_conv_kernel(u_ref, w_ref, o_ref):
    c = pl.program_id(1)
    lp = u_ref.shape[0]
    t = lax.broadcasted_iota(I32, (lp, 1), 0) - PAD_ROWS
    u = jnp.where(t >= 0, u_ref[...], 0.0)
    w = w_ref[...]
    y = (u * w[3:4] + pltpu.roll(u, 1, 0) * w[2:3] + pltpu.roll(u, 2, 0) * w[1:2]
         + pltpu.roll(u, 3, 0) * w[0:1])
    y = _silu(y)
    nrm = y * lax.rsqrt(jnp.sum(y * y, -1, keepdims=True) + RMS_EPS)
    o_ref[...] = jnp.where(c < HEADS, nrm * (HEAD_DIM ** -0.5), jnp.where(c < 2 * HEADS, nrm, y))


def _gdn_conv(proj_main, conv_w, batch, lp):
    tp = proj_main.shape[0]
    first = D_POOL // HEAD_DIM
    return pl.pallas_call(
        _gdn_conv_kernel,
        grid=(batch, 3 * HEADS),
        in_specs=[pl.BlockSpec((lp, HEAD_DIM), lambda b, c: (b, first + c)),
                  pl.BlockSpec((GDN_CONV, HEAD_DIM), lambda b, c: (0, c))],
        out_specs=pl.BlockSpec((lp, HEAD_DIM), lambda b, c: (b, c)),
        out_shape=jax.ShapeDtypeStruct((tp, 3 * D_HEADS), F32),
        compiler_params=_cparams(("parallel", "parallel"), 48),
        name="gdn_conv",
    )(proj_main, conv_w)


def _gdn_chunk_kernel(q_ref, k_ref, v_ref, z_ref, ab_ref, alog_ref, dtb_ref, ng_ref, o_ref, s_ref):
    c = pl.program_id(1)
    C = GDN_CHUNK

    @pl.when(c == 0)
    def _():
        s_ref[...] = jnp.zeros_like(s_ref)

    ab = ab_ref[...]
    pos = c * C + lax.broadcasted_iota(I32, (C, 1), 0)
    real = pos >= PAD_ROWS
    g_all = jnp.where(real, -jnp.exp(alog_ref[...]) * _softplus(ab + dtb_ref[...]), 0.0)
    beta_all = jnp.where(real, jax.nn.sigmoid(ab), 0.0)
    ri = lax.broadcasted_iota(I32, (C, C), 0)
    ci = lax.broadcasted_iota(I32, (C, C), 1)
    causal = ri >= ci
    strict = ri > ci
    tri = causal.astype(BF16)
    eye = (ri == ci).astype(F32)
    gcum = sum(jnp.dot(tri, p, preferred_element_type=F32) for p in _split_bf16(g_all, 3))
    gcum_t = jnp.concatenate([gcum, jnp.zeros_like(gcum)], axis=0).T
    ng = ng_ref[...]
    heads = range(HEADS)
    sls = [slice(h * HEAD_DIM, (h + 1) * HEAD_DIM) for h in heads]

    gc_col = [gcum[:, h:h + 1] for h in heads]
    g_last = [gcum[C - 1:C, h:h + 1] for h in heads]
    beta = [beta_all[:, HEADS + h:HEADS + h + 1] for h in heads]
    decay = [jnp.where(causal, jnp.exp(jnp.where(causal, gc_col[h] - gcum_t[h:h + 1, 0:C], 0.0)), 0.0)
             for h in heads]
    eg = [jnp.exp(gc_col[h]) for h in heads]
    qb = [q_ref[:, sls[h]].astype(BF16) for h in heads]
    kf = [k_ref[:, sls[h]] for h in heads]
    kb = [kf[h].astype(BF16) for h in heads]
    kbeta = [kf[h] * beta[h] for h in heads]
    lmat = [jnp.where(strict, _bdot_nt(kbeta[h], kb[h]) * decay[h], 0.0) for h in heads]
    attn = [_bdot_nt(qb[h], kb[h]) * decay[h] for h in heads]
    tinv = [eye - lmat[h] for h in heads]
    pw = [_bdot(lmat[h], lmat[h]) for h in heads]
    for it in range(5):
        tinv = [tinv[h] + _bdot(tinv[h], pw[h]) for h in heads]
        if it < 4:
            pw = [_bdot(pw[h], pw[h]) for h in heads]
    sol = [_bdot(tinv[h], jnp.concatenate([v_ref[:, sls[h]] * beta[h], kbeta[h] * eg[h]], axis=-1))
           for h in heads]
    state = [s_ref[h] for h in heads]
    ws = [_bdot(jnp.concatenate([sol[h][:, HEAD_DIM:], q_ref[:, sls[h]] * eg[h]], axis=0), state[h])
          for h in heads]
    v_new = [sol[h][:, :HEAD_DIM] - ws[h][:C] for h in heads]
    k_dec_t = [(kf[h] * jnp.exp(g_last[h] - gc_col[h])).T for h in heads]
    out = [ws[h][C:] + _bdot(attn[h], v_new[h]) for h in heads]
    for h in heads:
        s_ref[h] = state[h] * jnp.exp(g_last[h]) + _bdot(k_dec_t[h], v_new[h])
    for h in heads:
        o = out[h]
        o = o * lax.rsqrt(jnp.mean(o * o, -1, keepdims=True) + RMS_EPS) * ng
        o_ref[:, sls[h]] = (o * _silu(z_ref[:, sls[h]])).astype(o_ref.dtype)


def _gdn(qkv, proj_main, proj_side, a_log, dt_bias, norm_g, batch, lp):
    tp = qkv.shape[0]
    nc = lp // GDN_CHUNK
    zcol = (D_POOL + 3 * D_HEADS) // D_HEADS
    alog_pad = jnp.zeros((1, LANES), F32).at[0, :HEADS].set(a_log)
    dtb_pad = jnp.zeros((1, LANES), F32).at[0, :HEADS].set(dt_bias)
    row = lambda b, c: b * nc + c
    return pl.pallas_call(
        _gdn_chunk_kernel,
        grid=(batch, nc),
        in_specs=[pl.BlockSpec((GDN_CHUNK, D_HEADS), lambda b, c: (row(b, c), 0)),
                  pl.BlockSpec((GDN_CHUNK, D_HEADS), lambda b, c: (row(b, c), 1)),
                  pl.BlockSpec((GDN_CHUNK, D_HEADS), lambda b, c: (row(b, c), 2)),
                  pl.BlockSpec((GDN_CHUNK, D_HEADS), lambda b, c: (row(b, c), zcol)),
                  pl.BlockSpec((GDN_CHUNK, LANES), lambda b, c: (row(b, c), N_SB // LANES)),
                  pl.BlockSpec((1, LANES), lambda b, c: (0, 0)),
                  pl.BlockSpec((1, LANES), lambda b, c: (0, 0)),
                  pl.BlockSpec((1, HEAD_DIM), lambda b, c: (0, 0))],
        out_specs=pl.BlockSpec((GDN_CHUNK, D_HEADS), lambda b, c: (row(b, c), 0)),
        out_shape=jax.ShapeDtypeStruct((tp, D_HEADS), BF16),
        scratch_shapes=[pltpu.VMEM((HEADS, HEAD_DIM, HEAD_DIM), F32)],
        compiler_params=_cparams(("parallel", "arbitrary"), 48),
        name="gdn_chunks",
    )(qkv, qkv, qkv, proj_main, proj_side, alog_pad, dtb_pad, norm_g.reshape(1, HEAD_DIM))


SB_HEADS_PER_STEP = 4


def _sb_kernel(q_ref, k_ref, v_ref, o_ref, kb_ref, vb_ref, acc_ref):
    i = pl.program_id(2)
    T = ROW_BLOCK
    lp = k_ref.shape[0]
    heads = range(SB_HEADS_PER_STEP)
    sls = [slice(h * HEAD_DIM, (h + 1) * HEAD_DIM) for h in heads]

    @pl.when(i == 0)
    def _():
        real = lax.broadcasted_iota(I32, (lp, 1), 0) >= PAD_ROWS
        kb_ref[...] = k_ref[...].astype(BF16)
        vb_ref[...] = jnp.where(real, v_ref[...], 0.0).astype(BF16)

    q = [(q_ref[:, sls[h]] * (HEAD_DIM ** -0.5)).astype(BF16) for h in heads]
    rowi = lax.broadcasted_iota(I32, (T, T), 0)
    coli = lax.broadcasted_iota(I32, (T, T), 1)
    earlier = coli < rowi
    suffix = (rowi > coli).astype(BF16)

    def key_block(j, later, diagonal):
        start = pl.multiple_of(j * T, T)
        z = [_bdot_nt(q[h], kb_ref[pl.ds(start, T), sls[h]]) for h in heads]
        sp = [_softplus(z[h]) for h in heads]
        lom = [jnp.where(earlier, -sp[h], 0.0) if diagonal else -sp[h] for h in heads]
        parts = [_split_bf16(lom[h], 2) for h in heads]
        between = [later[h] + jnp.dot(parts[h][0], suffix, preferred_element_type=F32)
                   + jnp.dot(parts[h][1], suffix, preferred_element_type=F32) for h in heads]
        attn = [jnp.exp(z[h] - sp[h] + between[h]) for h in heads]
        if diagonal:
            attn = [jnp.where(earlier, attn[h], 0.0) for h in heads]
        pv = [_bdot(attn[h], vb_ref[pl.ds(start, T), sls[h]]) for h in heads]
        for h in heads:
            if diagonal:
                acc_ref[h] = pv[h]
            else:
                acc_ref[h] += pv[h]
        return tuple(later[h] + jnp.sum(lom[h], -1, keepdims=True) for h in heads)

    later = key_block(i, tuple(jnp.zeros((T, 1), F32) for _ in heads), True)
    lax.fori_loop(1, i + 1, lambda n, later: key_block(i - n, later, False), later)
    for h in heads:
        o_ref[:, sls[h]] = acc_ref[h].astype(o_ref.dtype)


def _stick_breaking(proj_side, batch, lp):
    tp = proj_side.shape[0]
    nq = lp // ROW_BLOCK
    hb = SB_HEADS_PER_STEP
    w = hb * HEAD_DIM
    ng = HEADS // hb
    return pl.pallas_call(
        _sb_kernel,
        grid=(batch, ng, nq),
        in_specs=[pl.BlockSpec((ROW_BLOCK, w), lambda b, g, i: (b * nq + i, g)),
                  pl.BlockSpec((lp, w), lambda b, g, i: (b, ng + g)),
                  pl.BlockSpec((lp, w), lambda b, g, i: (b, 2 * ng + g))],
        out_specs=pl.BlockSpec((ROW_BLOCK, w), lambda b, g, i: (b * nq + i, g)),
        out_shape=jax.ShapeDtypeStruct((tp, D_HEADS), BF16),
        scratch_shapes=[pltpu.VMEM((lp, w), BF16), pltpu.VMEM((lp, w), BF16),
                        pltpu.VMEM((hb, ROW_BLOCK, HEAD_DIM), F32)],
        compiler_params=_cparams(("parallel", "parallel", "arbitrary"), 48),
        name="stick_breaking",
    )(proj_side, proj_side, proj_side)


def _merge_kernel(h_ref, ya_ref, yb_ref, yc_ref, wg0_ref, wg1_ref, wg2_ref, bg0_ref, bg1_ref, bg2_ref,
                  wb_ref, o_ref, wg_s, wb_s):
    @pl.when(pl.program_id(1) == 0)
    def _():
        wg_s[0] = wg0_ref[...].astype(BF16)
        wg_s[1] = wg1_ref[...].astype(BF16)
        wg_s[2] = wg2_ref[...].astype(BF16)
        wb_s[...] = wb_ref[...].astype(BF16)

    h = h_ref[...]
    acc = None
    for n, (y_ref, bg_ref) in enumerate(((ya_ref, bg0_ref), (yb_ref, bg1_ref), (yc_ref, bg2_ref))):
        gate = jax.nn.sigmoid(jnp.dot(h, wg_s[n], preferred_element_type=F32) + bg_ref[...])
        up = jnp.dot(y_ref[...], wb_s[n], preferred_element_type=F32)
        acc = gate * up if acc is None else acc + gate * up
    o_ref[...] = acc.astype(o_ref.dtype)


def _merge(h16, ya, yb, yc, w_gate, b_gate, w_branch, l, tn=256, tm=512):
    tp = h16.shape[0]
    nj = D_MODEL // tn
    wg_spec = lambda n: pl.BlockSpec((None, D_MODEL, tn), lambda j, i: (l, 0, n * nj + j))
    bg_spec = lambda n: pl.BlockSpec((None, 1, tn), lambda j, i: (l, 0, n * nj + j))
    y_spec = pl.BlockSpec((tm, D_HEADS), lambda j, i: (i, 0))
    return pl.pallas_call(
        _merge_kernel,
        grid=(nj, tp // tm),
        in_specs=[pl.BlockSpec((tm, D_MODEL), lambda j, i: (i, 0)), y_spec, y_spec, y_spec,
                  wg_spec(0), wg_spec(1), wg_spec(2), bg_spec(0), bg_spec(1), bg_spec(2),
                  pl.BlockSpec((None, N_BRANCH, D_HEADS, tn), lambda j, i: (l, 0, 0, j))],
        out_specs=pl.BlockSpec((tm, tn), lambda j, i: (i, j)),
        out_shape=jax.ShapeDtypeStruct((tp, D_MODEL), BF16),
        scratch_shapes=[pltpu.VMEM((N_BRANCH, D_MODEL, tn), BF16),
                        pltpu.VMEM((N_BRANCH, D_HEADS, tn), BF16)],
        compiler_params=_cparams(("arbitrary", "arbitrary"), 56),
        name="branch_merge",
    )(h16, ya, yb, yc, w_gate, w_gate, w_gate, b_gate, b_gate, b_gate, w_branch)


ROW_TILES = D_MODEL // LANES


def _store_row_linear(ref, x):
    rows = x.shape[0]
    for c in range(ROW_TILES):
        ref[pl.ds(c, rows, stride=ROW_TILES), :] = x[:, c * LANES:(c + 1) * LANES]


def _load_row_linear(ref, first_row, rows):
    return jnp.concatenate([ref[pl.ds(first_row * ROW_TILES + c, rows, stride=ROW_TILES), :]
                            for c in range(ROW_TILES)], axis=-1)


def _oproj_ln_kernel(m_ref, w_ref, h_ref, g_ref, b_ref, ho_ref, h16o_ref, hlin_ref):
    mix = jnp.dot(m_ref[...], w_ref[...], preferred_element_type=F32)
    y = _ln_rows(DEEPNORM_ALPHA * h_ref[...] + mix, g_ref[...], b_ref[...])
    ho_ref[...] = y
    h16o_ref[...] = y.astype(BF16)
    _store_row_linear(hlin_ref, y)


def _oproj_ln(merged, w_o16, h, g, b, tm=256):
    tp, d = h.shape
    row = pl.BlockSpec((tm, d), lambda i: (i, 0))
    vec = pl.BlockSpec((1, d), lambda i: (0, 0))
    return pl.pallas_call(
        _oproj_ln_kernel,
        grid=(tp // tm,),
        in_specs=[row, pl.BlockSpec((d, d), lambda i: (0, 0)), row, vec, vec],
        out_specs=[row, row, pl.BlockSpec((tm * ROW_TILES, LANES), lambda i: (i, 0))],
        out_shape=[jax.ShapeDtypeStruct((tp, d), F32), jax.ShapeDtypeStruct((tp, d), BF16),
                   jax.ShapeDtypeStruct((tp * ROW_TILES, LANES), F32)],
        compiler_params=_cparams(("parallel",), 48),
        name="oproj_ln",
    )(merged, w_o16, h, g.reshape(1, d), b.reshape(1, d))


def _router_kernel(h_ref, w_ref, b_ref, idx_ref, wts_ref, rank_ref, cnt_ref, base_s, *, blocks_per_seq):
    i = pl.program_id(0)
    tm = h_ref.shape[0]

    @pl.when(i == 0)
    def _():
        base_s[...] = jnp.zeros_like(base_s)

    xh, xl = _split_bf16(h_ref[...], 2)
    wh, wl = _split_bf16(w_ref[...], 2)
    dot = functools.partial(jnp.dot, preferred_element_type=F32)
    scores = jax.nn.sigmoid(dot(xh, wh) + dot(xh, wl) + dot(xl, wh))
    lane = lax.broadcasted_iota(I32, (tm, LANES), 1).astype(F32)
    sel = jnp.where(lane < N_EXPERTS, scores + b_ref[...], NEG_INF)
    row = lax.broadcasted_iota(I32, (tm, 1), 0)
    real = jnp.logical_not((i % blocks_per_seq == 0) & (row < PAD_ROWS))
    chosen = jnp.zeros((tm, LANES), F32)
    idx_out = jnp.zeros((tm, LANES), F32)
    sc_out = jnp.zeros((tm, LANES), F32)
    picks = []
    for k in range(TOP_K):
        m = jnp.max(sel, -1, keepdims=True)
        ik = jnp.min(jnp.where(sel == m, lane, float(LANES)), -1, keepdims=True)
        hit = lane == ik
        sk = jnp.sum(jnp.where(hit, scores, 0.0), -1, keepdims=True)
        sel = jnp.where(hit, NEG_INF, sel)
        chosen = jnp.where(hit, 1.0, chosen)
        idx_out = jnp.where(lane == k, ik, idx_out)
        sc_out = jnp.where(lane == k, sk, sc_out)
        picks.append(hit)
    total = jnp.sum(sc_out, -1, keepdims=True)
    wts_ref[...] = jnp.where(real, sc_out / total * ROUTED_SCALE, 0.0)
    idx_ref[...] = idx_out.astype(I32)
    chosen = jnp.where(real, chosen, 0.0)
    before = (lax.broadcasted_iota(I32, (tm, tm), 0) > lax.broadcasted_iota(I32, (tm, tm), 1)).astype(BF16)
    count_before = dot(before, chosen.astype(BF16)) + base_s[0:1, :]
    rank_out = jnp.zeros((tm, LANES), F32)
    for k in range(TOP_K):
        rk = jnp.sum(jnp.where(picks[k], count_before, 0.0), -1, keepdims=True)
        rank_out = jnp.where(lane == k, rk, rank_out)
    rank_ref[...] = rank_out.astype(I32)
    base_s[...] = base_s[...] + jnp.sum(chosen, 0, keepdims=True)
    cnt_ref[...] = base_s[...].astype(I32)


def _router(h, router_w, router_bias, lp, tm=ROW_BLOCK):
    tp, d = h.shape
    w_pad = jnp.zeros((d, LANES), F32).at[:, :N_EXPERTS].set(router_w)
    b_pad = jnp.zeros((1, LANES), F32).at[0, :N_EXPERTS].set(router_bias)
    tok = pl.BlockSpec((tm, LANES), lambda i: (i, 0))
    return pl.pallas_call(
        functools.partial(_router_kernel, blocks_per_seq=lp // tm),
        grid=(tp // tm,),
        in_specs=[pl.BlockSpec((tm, d), lambda i: (i, 0)),
                  pl.BlockSpec((d, LANES), lambda i: (0, 0)),
                  pl.BlockSpec((1, LANES), lambda i: (0, 0))],
        out_specs=[tok, tok, tok, pl.BlockSpec((8, LANES), lambda i: (0, 0))],
        out_shape=[jax.ShapeDtypeStruct((tp, LANES), I32), jax.ShapeDtypeStruct((tp, LANES), F32),
                   jax.ShapeDtypeStruct((tp, LANES), I32), jax.ShapeDtypeStruct((8, LANES), I32)],
        scratch_shapes=[pltpu.VMEM((8, LANES), F32)],
        compiler_params=_cparams(("arbitrary",)),
        name="moe_router",
    )(h, w_pad, b_pad)


GATHER_UNROLL = 8


def _row_gather_start(idx_ref, n_rows, src_hbm, dst, sem):
    def body(g, carry):
        for u in range(GATHER_UNROLL):
            r = g * GATHER_UNROLL + u
            t = idx_ref[0, 0, r]
            pltpu.make_async_copy(src_hbm.at[pl.ds(pl.multiple_of(t * ROW_TILES, ROW_TILES), ROW_TILES), :],
                                  dst.at[pl.ds(pl.multiple_of(r * ROW_TILES, ROW_TILES), ROW_TILES), :],
                                  sem).start()
        return carry

    lax.fori_loop(0, n_rows // GATHER_UNROLL, body, 0)


def _row_gather_wait(n_rows, src_hbm, dst, sem):
    pltpu.make_async_copy(src_hbm.at[pl.ds(0, n_rows * ROW_TILES), :], dst, sem).wait()


def _ffn_kernel(be_ref, nu_ref, tokc_ref, tokn_ref, h_hbm, wg_ref, wu_ref, wd_ref, y_ref,
                xbuf, sem, wg_s, wu_s, wd_s):
    i = pl.program_id(0)
    n_used = nu_ref[0]
    slot = i % 2

    @pl.when(i == 0)
    def _():
        _row_gather_start(tokc_ref, ROW_BLOCK, h_hbm, xbuf.at[0], sem.at[0])

    @pl.when(i < n_used)
    def _():
        _row_gather_wait(ROW_BLOCK, h_hbm, xbuf.at[slot], sem.at[slot])

    @pl.when(i + 1 < n_used)
    def _():
        _row_gather_start(tokn_ref, ROW_BLOCK, h_hbm, xbuf.at[1 - slot], sem.at[1 - slot])

    prev = be_ref[jnp.maximum(i - 1, 0)]

    @pl.when((i == 0) | (be_ref[i] != prev))
    def _():
        wg_s[...] = wg_ref[...].astype(BF16)
        wu_s[...] = wu_ref[...].astype(BF16)
        wd_s[...] = wd_ref[...].astype(BF16)

    @pl.when(i < n_used)
    def _():
        x = _load_row_linear(xbuf.at[slot], 0, ROW_BLOCK).astype(BF16)
        hid = _silu(jnp.dot(x, wg_s[...], preferred_element_type=F32)) * jnp.dot(
            x, wu_s[...], preferred_element_type=F32)
        _store_row_linear(y_ref, jnp.dot(hid.astype(BF16), wd_s[...], preferred_element_type=F32))

    @pl.when(i >= n_used)
    def _():
        y_ref[...] = jnp.zeros_like(y_ref)


def _expert_ffn(h_lin, row_tok, block_e, n_used, e_gate, e_up, e_down, l):
    d = D_MODEL
    nb = block_e.shape[0]
    tok3 = row_tok.reshape(nb, 1, ROW_BLOCK)
    smem_blk = lambda f: pl.BlockSpec((1, 1, ROW_BLOCK), f, memory_space=pltpu.SMEM)
    grid_spec = pltpu.PrefetchScalarGridSpec(
        num_scalar_prefetch=2,
        grid=(nb,),
        in_specs=[smem_blk(lambda i, be, nu: (i, 0, 0)),
                  smem_blk(lambda i, be, nu: (jnp.minimum(i + 1, nb - 1), 0, 0)),
                  pl.BlockSpec(memory_space=pl.ANY),
                  pl.BlockSpec((None, None, d, D_EXPERT), lambda i, be, nu: (l, be[i], 0, 0)),
                  pl.BlockSpec((None, None, d, D_EXPERT), lambda i, be, nu: (l, be[i], 0, 0)),
                  pl.BlockSpec((None, None, D_EXPERT, d), lambda i, be, nu: (l, be[i], 0, 0))],
        out_specs=pl.BlockSpec((ROW_BLOCK * ROW_TILES, LANES), lambda i, be, nu: (i, 0)),
        scratch_shapes=[pltpu.VMEM((2, ROW_BLOCK * ROW_TILES, LANES), F32),
                        pltpu.SemaphoreType.DMA((2,)),
                        pltpu.VMEM((d, D_EXPERT), BF16),
                        pltpu.VMEM((d, D_EXPERT), BF16),
                        pltpu.VMEM((D_EXPERT, d), BF16)],
    )
    return pl.pallas_call(
        _ffn_kernel,
        grid_spec=grid_spec,
        out_shape=jax.ShapeDtypeStruct((nb * ROW_BLOCK * ROW_TILES, LANES), F32),
        compiler_params=_cparams(("arbitrary",), 48),
        name="expert_ffn",
    )(block_e, n_used, tok3, tok3, h_lin, e_gate, e_up, e_down)


def _combine_kernel(dc_ref, dn_ref, y_hbm, h_ref, h16_ref, wts_ref, sg_ref, su_ref, sd_ref, g_ref, b_ref,
                    ho_ref, h16o_ref, ybuf, sem):
    i = pl.program_id(0)
    n = pl.num_programs(0)
    tm = h_ref.shape[0]
    slot = i % 2

    @pl.when(i == 0)
    def _():
        _row_gather_start(dc_ref, TOP_K * tm, y_hbm, ybuf.at[0], sem.at[0])

    _row_gather_wait(TOP_K * tm, y_hbm, ybuf.at[slot], sem.at[slot])

    @pl.when(i + 1 < n)
    def _():
        _row_gather_start(dn_ref, TOP_K * tm, y_hbm, ybuf.at[1 - slot], sem.at[1 - slot])

    x = h16_ref[...]
    hid = _silu(jnp.dot(x, sg_ref[...], preferred_element_type=F32)) * jnp.dot(
        x, su_ref[...], preferred_element_type=F32)
    acc = jnp.dot(hid.astype(BF16), sd_ref[...], preferred_element_type=F32)
    wts = wts_ref[...]
    for k in range(TOP_K):
        acc = acc + _load_row_linear(ybuf.at[slot], k * tm, tm) * wts[:, k:k + 1]
    y = _ln_rows(DEEPNORM_ALPHA * h_ref[...] + acc, g_ref[...], b_ref[...])
    ho_ref[...] = y
    h16o_ref[...] = y.astype(BF16)


def _combine(y_lin, dest, h, h16, wts, sg16, su16, sd16, g, b, tm=ROW_BLOCK):
    tp, d = h.shape
    nt = tp // tm
    dest3 = dest.reshape(nt, tm, TOP_K).transpose(0, 2, 1).reshape(nt, 1, TOP_K * tm)
    smem_blk = lambda f: pl.BlockSpec((1, 1, TOP_K * tm), f, memory_space=pltpu.SMEM)
    row = pl.BlockSpec((tm, d), lambda i: (i, 0))
    vec = pl.BlockSpec((1, d), lambda i: (0, 0))
    return pl.pallas_call(
        _combine_kernel,
        grid=(nt,),
        in_specs=[smem_blk(lambda i: (i, 0, 0)),
                  smem_blk(lambda i: (jnp.minimum(i + 1, nt - 1), 0, 0)),
                  pl.BlockSpec(memory_space=pl.ANY),
                  row, row,
                  pl.BlockSpec((tm, LANES), lambda i: (i, 0)),
                  pl.BlockSpec((d, D_EXPERT), lambda i: (0, 0)),
                  pl.BlockSpec((d, D_EXPERT), lambda i: (0, 0)),
                  pl.BlockSpec((D_EXPERT, d), lambda i: (0, 0)),
                  vec, vec],
        out_specs=[row, row],
        out_shape=[jax.ShapeDtypeStruct((tp, d), F32), jax.ShapeDtypeStruct((tp, d), BF16)],
        scratch_shapes=[pltpu.VMEM((2, TOP_K * tm * ROW_TILES, LANES), F32), pltpu.SemaphoreType.DMA((2,))],
        compiler_params=_cparams(("arbitrary",), 56),
        name="moe_combine",
    )(dest3, dest3, y_lin, h, h16, wts, sg16, su16, sd16, g.reshape(1, d), b.reshape(1, d))


def _routing_tables(idx, rank, counts, lp):
    tp = idx.shape[0]
    n_blocks = -(-(tp * TOP_K) // ROW_BLOCK) + N_EXPERTS
    n_rows = n_blocks * ROW_BLOCK
    padded = (counts + ROW_BLOCK - 1) // ROW_BLOCK * ROW_BLOCK
    pad_end = jnp.cumsum(padded)
    pad_start = pad_end - padded
    real = (jnp.arange(tp, dtype=I32) % lp) >= PAD_ROWS
    dest = pad_start[idx] + rank
    scatter_to = jnp.where(real[:, None], dest, n_rows).reshape(-1)
    tok = jnp.repeat(jnp.arange(tp, dtype=I32), TOP_K)
    row_tok = jnp.zeros((n_rows,), I32).at[scatter_to].set(tok, mode="drop")
    block_start = jnp.arange(n_blocks, dtype=I32) * ROW_BLOCK
    block_e = jnp.minimum(jnp.sum((pad_end[None, :] <= block_start[:, None]).astype(I32), axis=1), N_EXPERTS - 1)
    n_used = (pad_end[-1] // ROW_BLOCK).astype(I32).reshape(1)
    return row_tok, block_e, n_used, jnp.where(real[:, None], dest, 0)


def kernel(x, meta, ln_in_g, ln_in_b, w_in, pool_w, pool_scale, conv_w, A_log, dt_bias, gdn_norm_g, w_branch, w_gate, b_gate, w_o, ln1_g, ln1_b, router_w, router_bias, exp_w_gate, exp_w_up, exp_w_down, sh_w_gate, sh_w_up, sh_w_down, ln2_g, ln2_b):
    batch, seq, d = x.shape
    depth = w_in.shape[0]
    lp = PAD_ROWS + N_META + seq
    assert seq % ROW_BLOCK == 0 and d == D_MODEL
    tp = batch * lp
    xp = jnp.concatenate([jnp.zeros((batch, PAD_ROWS, d), x.dtype),
                          jnp.broadcast_to(meta[None].astype(x.dtype), (batch, N_META, d)), x], axis=1)
    h, h16 = _ln_in(xp.reshape(tp, d), ln_in_g, ln_in_b)
    b_gate3 = b_gate.reshape(depth, 1, N_BRANCH * d)
    tm = 512 if tp % 512 == 0 else ROW_BLOCK

    for l in range(depth):
        w_l = w_in[l]
        w_side = jnp.concatenate([w_l[:, N_MAIN + N_GATES:], w_l[:, N_MAIN:N_MAIN + N_GATES],
                                  jnp.zeros((d, LANES - N_GATES), F32)], axis=1)
        proj_main = _matmul(h16, w_in, pl.BlockSpec((None, d, 512), lambda j, i: (l, 0, j)),
                            N_MAIN, 512, tm, name="proj_main")
        proj_side = _matmul(h16, w_side, pl.BlockSpec((d, 640), lambda j, i: (0, j)),
                            N_SIDE, 640, tm, name="proj_side")
        y_a = _pool_mixer(proj_main, pool_w[l], pool_scale[l], batch, lp)
        qkv = _gdn_conv(proj_main, conv_w[l], batch, lp)
        y_b = _gdn(qkv, proj_main, proj_side, A_log[l], dt_bias[l], gdn_norm_g[l], batch, lp)
        y_c = _stick_breaking(proj_side, batch, lp)
        merged = _merge(h16, y_a, y_b, y_c, w_gate, b_gate3, w_branch, l, tm=tm)
        h, h16, h_lin = _oproj_ln(merged, w_o[l].astype(BF16), h, ln1_g[l], ln1_b[l])

        idx, wts, rank, cnt = _router(h, router_w[l], router_bias[l], lp)
        row_tok, block_e, n_used, dest = _routing_tables(idx[:, :TOP_K], rank[:, :TOP_K], cnt[0, :N_EXPERTS], lp)
        y_lin = _expert_ffn(h_lin, row_tok, block_e, n_used, exp_w_gate, exp_w_up, exp_w_down, l)
        h, h16 = _combine(y_lin, dest, h, h16, wts, sh_w_gate[l].astype(BF16), sh_w_up[l].astype(BF16),
                          sh_w_down[l].astype(BF16), ln2_g[l], ln2_b[l])

    return h.reshape(batch, lp, d)[:, PAD_ROWS + N_META:]
```

```python
import functools
import math

import jax
import jax.numpy as jnp
from jax import lax
from jax.experimental import pallas as pl
from jax.experimental.pallas import tpu as pltpu

F32 = jnp.float32
BF16 = jnp.bfloat16
I32 = jnp.int32

D_MODEL = 2048
DEPTH = 4
N_META = 16
ROW_BLOCK = 128
PAD_ROWS = ROW_BLOCK - N_META
POOL_GROUP = 256
N_POOL_GROUPS = 4
D_POOL = POOL_GROUP * N_POOL_GROUPS
HEADS = 8
HEAD_DIM = 128
D_HEADS = HEADS * HEAD_DIM
GDN_CONV = 4
GDN_CHUNK = 64
N_BRANCH = 3
N_EXPERTS = 64
TOP_K = 8
D_EXPERT = 384
ROUTED_SCALE = 2.5
DEEPNORM_ALPHA = (2 * DEPTH) ** 0.25
LN_EPS = 1e-5
RMS_EPS = 1e-6
LANES = 128
N_MAIN = D_POOL + 4 * D_HEADS
N_GATES = 2 * HEADS
N_SB = 3 * D_HEADS
N_SIDE = N_SB + LANES
NEG_INF = float("-inf")


def _cparams(semantics, vmem_mib=None):
    kw = {}
    if vmem_mib is not None:
        kw["vmem_limit_bytes"] = vmem_mib * 2**20
    return pltpu.CompilerParams(dimension_semantics=semantics, **kw)


def _bdot(a, b):
    return jnp.dot(a.astype(BF16), b.astype(BF16), preferred_element_type=F32)


def _bdot_nt(a, b):
    return lax.dot_general(a.astype(BF16), b.astype(BF16), (((1,), (1,)), ((), ())),
                           preferred_element_type=F32)


def _split_bf16(x, parts):
    out, r = [], x
    for _ in range(parts):
        p = r.astype(BF16)
        out.append(p)
        r = r - p.astype(F32)
    return out


def _ln_rows(x, g, b):
    mu = jnp.mean(x, -1, keepdims=True)
    xc = x - mu
    var = jnp.mean(xc * xc, -1, keepdims=True)
    return xc * lax.rsqrt(var + LN_EPS) * g + b


def _silu(x):
    return x * jax.nn.sigmoid(x)


def _softplus(x):
    return jnp.maximum(x, 0.0) + jnp.log1p(jnp.exp(-jnp.abs(x)))


def _ln_in_kernel(x_ref, meta_ref, g_ref, b_ref, h_ref, h16_ref, *, blocks_per_seq):
    first = pl.program_id(0) % blocks_per_seq == 0
    d = x_ref.shape[1]
    lead = jnp.concatenate([jnp.zeros((PAD_ROWS, d), F32), meta_ref[...]], axis=0)
    y = _ln_rows(jnp.where(first, lead, x_ref[...]), g_ref[...], b_ref[...])
    h_ref[...] = y
    h16_ref[...] = y.astype(BF16)


def _frame_block(i, blocks_per_seq):
    b = i // blocks_per_seq
    j = i - b * blocks_per_seq
    return b * (blocks_per_seq - 1) + jnp.maximum(j - 1, 0)


def _ln_in(x2, meta, g, b, lp):
    n, d = x2.shape
    tm = ROW_BLOCK
    bps = lp // tm
    tp = n // (bps - 1) // tm * lp
    row = pl.BlockSpec((tm, d), lambda i: (i, 0))
    vec = pl.BlockSpec((1, d), lambda i: (0, 0))
    return pl.pallas_call(
        functools.partial(_ln_in_kernel, blocks_per_seq=bps),
        grid=(tp // tm,),
        in_specs=[pl.BlockSpec((tm, d), lambda i: (_frame_block(i, bps), 0)),
                  pl.BlockSpec((N_META, d), lambda i: (0, 0)), vec, vec],
        out_specs=[row, row],
        out_shape=[jax.ShapeDtypeStruct((tp, d), F32), jax.ShapeDtypeStruct((tp, d), BF16)],
        compiler_params=_cparams(("arbitrary",)),
        name="ln_in",
    )(x2, meta, g.reshape(1, d), b.reshape(1, d))


def _mm_kernel(x_ref, w_ref, o_ref, wb_ref):
    @pl.when(pl.program_id(1) == 0)
    def _():
        wb_ref[...] = w_ref[...].astype(BF16)

    o_ref[...] = jnp.dot(x_ref[...], wb_ref[...], preferred_element_type=F32).astype(o_ref.dtype)


def _matmul(x16, w, w_spec, n_out, tn, tm=512, name="mm"):
    m, k = x16.shape
    return pl.pallas_call(
        _mm_kernel,
        grid=(n_out // tn, m // tm),
        in_specs=[pl.BlockSpec((tm, k), lambda j, i: (i, 0)), w_spec],
        out_specs=pl.BlockSpec((tm, tn), lambda j, i: (i, j)),
        out_shape=jax.ShapeDtypeStruct((m, n_out), F32),
        scratch_shapes=[pltpu.VMEM((k, tn), BF16)],
        compiler_params=_cparams(("arbitrary", "arbitrary"), 48),
        name=name,
    )(x16, w)


def _pool_kernel(u_ref, w_ref, s_ref, o_ref):
    g = pl.program_id(1)
    lp = u_ref.shape[0]
    t = lax.broadcasted_iota(I32, (lp, 1), 0) - PAD_ROWS
    u = jnp.where(t >= 0, u_ref[...], 0.0)
    s2 = u + pltpu.roll(u, 1, 0)
    s4 = s2 + pltpu.roll(s2, 2, 0)
    s8 = s4 + pltpu.roll(s4, 4, 0)
    s16 = s8 + pltpu.roll(s8, 8, 0)
    s = jnp.where(g == 0, s2, jnp.where(g == 1, s4, jnp.where(g == 2, s8, s16)))
    window = jnp.left_shift(2, g)
    cnt = jnp.maximum(jnp.minimum(t + 1, window), 1).astype(F32)
    pooled = s / cnt - u
    y = _bdot(pooled, w_ref[...]) * s_ref[...]
    o_ref[...] = y.astype(o_ref.dtype)


def _pool_mixer(proj_main, pool_w, pool_scale, batch, lp):
    tp = proj_main.shape[0]
    return pl.pallas_call(
        _pool_kernel,
        grid=(batch, N_POOL_GROUPS),
        in_specs=[pl.BlockSpec((lp, POOL_GROUP), lambda b, g: (b, g)),
                  pl.BlockSpec((None, POOL_GROUP, POOL_GROUP), lambda b, g: (g, 0, 0)),
                  pl.BlockSpec((1, POOL_GROUP), lambda b, g: (0, g))],
        out_specs=pl.BlockSpec((lp, POOL_GROUP), lambda b, g: (b, g)),
        out_shape=jax.ShapeDtypeStruct((tp, D_POOL), BF16),
        compiler_params=_cparams(("parallel", "parallel"), 48),
        name="pool_mixer",
    )(proj_main, pool_w, pool_scale.reshape(1, D_POOL))


def _gdn_conv_kernel(u_ref, w_ref, o_ref):
    c = pl.program_id(1)
    lp = u_ref.shape[0]
    t = lax.broadcasted_iota(I32, (lp, 1), 0) - PAD_ROWS
    u = jnp.where(t >= 0, u_ref[...], 0.0)
    w = w_ref[...]
    y = (u * w[3:4] + pltpu.roll(u, 1, 0) * w[2:3] + pltpu.roll(u, 2, 0) * w[1:2]
         + pltpu.roll(u, 3, 0) * w[0:1])
    y = _silu(y)
    nrm = y * lax.rsqrt(jnp.sum(y * y, -1, keepdims=True) + RMS_EPS)
    o_ref[...] = jnp.where(c < HEADS, nrm * (HEAD_DIM ** -0.5), jnp.where(c < 2 * HEADS, nrm, y))


def _gdn_conv(proj_main, conv_w, batch, lp):
    tp = proj_main.shape[0]
    first = D_POOL // HEAD_DIM
    return pl.pallas_call(
        _gdn_conv_kernel,
        grid=(batch, 3 * HEADS),
        in_specs=[pl.BlockSpec((lp, HEAD_DIM), lambda b, c: (b, first + c)),
                  pl.BlockSpec((GDN_CONV, HEAD_DIM), lambda b, c: (0, c))],
        out_specs=pl.BlockSpec((lp, HEAD_DIM), lambda b, c: (b, c)),
        out_shape=jax.ShapeDtypeStruct((tp, 3 * D_HEADS), F32),
        compiler_params=_cparams(("parallel", "parallel"), 48),
        name="gdn_conv",
    )(proj_main, conv_w)


def _gdn_chunk_kernel(q_ref, k_ref, v_ref, z_ref, ab_ref, alog_ref, dtb_ref, ng_ref, o_ref, s_ref):
    c = pl.program_id(1)
    C = GDN_CHUNK

    @pl.when(c == 0)
    def _():
        s_ref[...] = jnp.zeros_like(s_ref)

    ab = ab_ref[...]
    pos = c * C + lax.broadcasted_iota(I32, (C, 1), 0)
    real = pos >= PAD_ROWS
    g_all = jnp.where(real, -jnp.exp(alog_ref[...]) * _softplus(ab + dtb_ref[...]), 0.0)
    beta_all = jnp.where(real, jax.nn.sigmoid(ab), 0.0)
    ri = lax.broadcasted_iota(I32, (C, C), 0)
    ci = lax.broadcasted_iota(I32, (C, C), 1)
    causal = ri >= ci
    strict = ri > ci
    tri = causal.astype(BF16)
    eye = (ri == ci).astype(F32)
    gcum = sum(jnp.dot(tri, p, preferred_element_type=F32) for p in _split_bf16(g_all, 3))
    gcum_t = jnp.concatenate([gcum, jnp.zeros_like(gcum)], axis=0).T
    ng = ng_ref[...]
    heads = range(HEADS)
    sls = [slice(h * HEAD_DIM, (h + 1) * HEAD_DIM) for h in heads]

    gc_col = [gcum[:, h:h + 1] for h in heads]
    g_last = [gcum[C - 1:C, h:h + 1] for h in heads]
    beta = [beta_all[:, HEADS + h:HEADS + h + 1] for h in heads]
    decay = [jnp.where(causal, jnp.exp(jnp.where(causal, gc_col[h] - gcum_t[h:h + 1, 0:C], 0.0)), 0.0)
             for h in heads]
    eg = [jnp.exp(gc_col[h]) for h in heads]
    qb = [q_ref[:, sls[h]].astype(BF16) for h in heads]
    kf = [k_ref[:, sls[h]] for h in heads]
    kb = [kf[h].astype(BF16) for h in heads]
    kbeta = [kf[h] * beta[h] for h in heads]
    lmat = [jnp.where(strict, _bdot_nt(kbeta[h], kb[h]) * decay[h], 0.0) for h in heads]
    attn = [_bdot_nt(qb[h], kb[h]) * decay[h] for h in heads]
    tinv = [eye - lmat[h] for h in heads]
    pw = [_bdot(lmat[h], lmat[h]) for h in heads]
    for it in range(5):
        tinv = [tinv[h] + _bdot(tinv[h], pw[h]) for h in heads]
        if it < 4:
            pw = [_bdot(pw[h], pw[h]) for h in heads]
    sol = [_bdot(tinv[h], jnp.concatenate([v_ref[:, sls[h]] * beta[h], kbeta[h] * eg[h]], axis=-1))
           for h in heads]
    state = [s_ref[h] for h in heads]
    ws = [_bdot(jnp.concatenate([sol[h][:, HEAD_DIM:], q_ref[:, sls[h]] * eg[h]], axis=0), state[h])
          for h in heads]
    v_new = [sol[h][:, :HEAD_DIM] - ws[h][:C] for h in heads]
    k_dec_t = [(kf[h] * jnp.exp(g_last[h] - gc_col[h])).T for h in heads]
    out = [ws[h][C:] + _bdot(attn[h], v_new[h]) for h in heads]
    for h in heads:
        s_ref[h] = state[h] * jnp.exp(g_last[h]) + _bdot(k_dec_t[h], v_new[h])
    for h in heads:
        o = out[h]
        o = o * lax.rsqrt(jnp.mean(o * o, -1, keepdims=True) + RMS_EPS) * ng
        o_ref[:, sls[h]] = (o * _silu(z_ref[:, sls[h]])).astype(o_ref.dtype)


def _gdn(qkv, proj_main, proj_side, a_log, dt_bias, norm_g, batch, lp):
    tp = qkv.shape[0]
    nc = lp // GDN_CHUNK
    zcol = (D_POOL + 3 * D_HEADS) // D_HEADS
    alog_pad = jnp.zeros((1, LANES), F32).at[0, :HEADS].set(a_log)
    dtb_pad = jnp.zeros((1, LANES), F32).at[0, :HEADS].set(dt_bias)
    row = lambda b, c: b * nc + c
    return pl.pallas_call(
        _gdn_chunk_kernel,
        grid=(batch, nc),
        in_specs=[pl.BlockSpec((GDN_CHUNK, D_HEADS), lambda b, c: (row(b, c), 0)),
                  pl.BlockSpec((GDN_CHUNK, D_HEADS), lambda b, c: (row(b, c), 1)),
                  pl.BlockSpec((GDN_CHUNK, D_HEADS), lambda b, c: (row(b, c), 2)),
                  pl.BlockSpec((GDN_CHUNK, D_HEADS), lambda b, c: (row(b, c), zcol)),
                  pl.BlockSpec((GDN_CHUNK, LANES), lambda b, c: (row(b, c), N_SB // LANES)),
                  pl.BlockSpec((1, LANES), lambda b, c: (0, 0)),
                  pl.BlockSpec((1, LANES), lambda b, c: (0, 0)),
                  pl.BlockSpec((1, HEAD_DIM), lambda b, c: (0, 0))],
        out_specs=pl.BlockSpec((GDN_CHUNK, D_HEADS), lambda b, c: (row(b, c), 0)),
        out_shape=jax.ShapeDtypeStruct((tp, D_HEADS), BF16),
        scratch_shapes=[pltpu.VMEM((HEADS, HEAD_DIM, HEAD_DIM), F32)],
        compiler_params=_cparams(("parallel", "arbitrary"), 48),
        name="gdn_chunks",
    )(qkv, qkv, qkv, proj_main, proj_side, alog_pad, dtb_pad, norm_g.reshape(1, HEAD_DIM))


SB_HEADS_PER_STEP = 8


def _sb_kernel(q_ref, k_ref, v_ref, o_ref, kb_ref, vb_ref, acc_ref):
    i = pl.program_id(2)
    T = ROW_BLOCK
    lp = k_ref.shape[0]
    heads = range(SB_HEADS_PER_STEP)
    sls = [slice(h * HEAD_DIM, (h + 1) * HEAD_DIM) for h in heads]

    @pl.when(i == 0)
    def _():
        real = lax.broadcasted_iota(I32, (lp, 1), 0) >= PAD_ROWS
        kb_ref[...] = k_ref[...].astype(BF16)
        vb_ref[...] = jnp.where(real, v_ref[...], 0.0).astype(BF16)

    q = [(q_ref[:, sls[h]] * (HEAD_DIM ** -0.5)).astype(BF16) for h in heads]
    rowi = lax.broadcasted_iota(I32, (T, T), 0)
    coli = lax.broadcasted_iota(I32, (T, T), 1)
    earlier = coli < rowi
    suffix = (rowi > coli).astype(BF16)

    def key_block(j, later, diagonal):
        start = pl.multiple_of(j * T, T)
        z = [_bdot_nt(q[h], kb_ref[pl.ds(start, T), sls[h]]) for h in heads]
        sp = [_softplus(z[h]) for h in heads]
        lom = [jnp.where(earlier, -sp[h], 0.0) if diagonal else -sp[h] for h in heads]
        parts = [_split_bf16(lom[h], 2) for h in heads]
        between = [later[h] + jnp.dot(parts[h][0], suffix, preferred_element_type=F32)
                   + jnp.dot(parts[h][1], suffix, preferred_element_type=F32) for h in heads]
        attn = [jnp.exp(z[h] - sp[h] + between[h]) for h in heads]
        if diagonal:
            attn = [jnp.where(earlier, attn[h], 0.0) for h in heads]
        pv = [_bdot(attn[h], vb_ref[pl.ds(start, T), sls[h]]) for h in heads]
        for h in heads:
            if diagonal:
                acc_ref[h] = pv[h]
            else:
                acc_ref[h] += pv[h]
        return tuple(later[h] + jnp.sum(lom[h], -1, keepdims=True) for h in heads)

    later = key_block(i, tuple(jnp.zeros((T, 1), F32) for _ in heads), True)
    lax.fori_loop(1, i + 1, lambda n, later: key_block(i - n, later, False), later)
    for h in heads:
        o_ref[:, sls[h]] = acc_ref[h].astype(o_ref.dtype)


def _stick_breaking(proj_side, batch, lp):
    tp = proj_side.shape[0]
    nq = lp // ROW_BLOCK
    hb = SB_HEADS_PER_STEP
    w = hb * HEAD_DIM
    ng = HEADS // hb
    return pl.pallas_call(
        _sb_kernel,
        grid=(batch, ng, nq),
        in_specs=[pl.BlockSpec((ROW_BLOCK, w), lambda b, g, i: (b * nq + i, g)),
                  pl.BlockSpec((lp, w), lambda b, g, i: (b, ng + g)),
                  pl.BlockSpec((lp, w), lambda b, g, i: (b, 2 * ng + g))],
        out_specs=pl.BlockSpec((ROW_BLOCK, w), lambda b, g, i: (b * nq + i, g)),
        out_shape=jax.ShapeDtypeStruct((tp, D_HEADS), BF16),
        scratch_shapes=[pltpu.VMEM((lp, w), BF16), pltpu.VMEM((lp, w), BF16),
                        pltpu.VMEM((hb, ROW_BLOCK, HEAD_DIM), F32)],
        compiler_params=_cparams(("parallel", "parallel", "arbitrary"), 58),
        name="stick_breaking",
    )(proj_side, proj_side, proj_side)


def _merge_kernel(h_ref, ya_ref, yb_ref, yc_ref, wg0_ref, wg1_ref, wg2_ref, bg0_ref, bg1_ref, bg2_ref,
                  wb_ref, o_ref, wg_s, wb_s):
    @pl.when(pl.program_id(1) == 0)
    def _():
        wg_s[0] = wg0_ref[...].astype(BF16)
        wg_s[1] = wg1_ref[...].astype(BF16)
        wg_s[2] = wg2_ref[...].astype(BF16)
        wb_s[...] = wb_ref[...].astype(BF16)

    h = h_ref[...]
    acc = None
    for n, (y_ref, bg_ref) in enumerate(((ya_ref, bg0_ref), (yb_ref, bg1_ref), (yc_ref, bg2_ref))):
        gate = jax.nn.sigmoid(jnp.dot(h, wg_s[n], preferred_element_type=F32) + bg_ref[...])
        up = jnp.dot(y_ref[...], wb_s[n], preferred_element_type=F32)
        acc = gate * up if acc is None else acc + gate * up
    o_ref[...] = acc.astype(o_ref.dtype)


def _merge(h16, ya, yb, yc, w_gate, b_gate, w_branch, l, tn=256, tm=512):
    tp = h16.shape[0]
    nj = D_MODEL // tn
    wg_spec = lambda n: pl.BlockSpec((None, D_MODEL, tn), lambda j, i: (l, 0, n * nj + j))
    bg_spec = lambda n: pl.BlockSpec((None, 1, tn), lambda j, i: (l, 0, n * nj + j))
    y_spec = pl.BlockSpec((tm, D_HEADS), lambda j, i: (i, 0))
    return pl.pallas_call(
        _merge_kernel,
        grid=(nj, tp // tm),
        in_specs=[pl.BlockSpec((tm, D_MODEL), lambda j, i: (i, 0)), y_spec, y_spec, y_spec,
                  wg_spec(0), wg_spec(1), wg_spec(2), bg_spec(0), bg_spec(1), bg_spec(2),
                  pl.BlockSpec((None, N_BRANCH, D_HEADS, tn), lambda j, i: (l, 0, 0, j))],
        out_specs=pl.BlockSpec((tm, tn), lambda j, i: (i, j)),
        out_shape=jax.ShapeDtypeStruct((tp, D_MODEL), BF16),
        scratch_shapes=[pltpu.VMEM((N_BRANCH, D_MODEL, tn), BF16),
                        pltpu.VMEM((N_BRANCH, D_HEADS, tn), BF16)],
        compiler_params=_cparams(("arbitrary", "arbitrary"), 56),
        name="branch_merge",
    )(h16, ya, yb, yc, w_gate, w_gate, w_gate, b_gate, b_gate, b_gate, w_branch)


ROW_TILES = D_MODEL // LANES


def _store_row_linear(ref, x):
    rows = x.shape[0]
    for c in range(ROW_TILES):
        ref[pl.ds(c, rows, stride=ROW_TILES), :] = x[:, c * LANES:(c + 1) * LANES]


def _load_row_linear(ref, first_row, rows):
    return jnp.concatenate([ref[pl.ds(first_row * ROW_TILES + c, rows, stride=ROW_TILES), :]
                            for c in range(ROW_TILES)], axis=-1)


def _oproj_ln_kernel(m_ref, w_ref, h_ref, g_ref, b_ref, ho_ref, h16o_ref, hlin_ref):
    mix = jnp.dot(m_ref[...], w_ref[...], preferred_element_type=F32)
    y = _ln_rows(DEEPNORM_ALPHA * h_ref[...] + mix, g_ref[...], b_ref[...])
    ho_ref[...] = y
    h16o_ref[...] = y.astype(BF16)
    _store_row_linear(hlin_ref, y)


def _oproj_ln(merged, w_o16, h, g, b, tm=256):
    tp, d = h.shape
    row = pl.BlockSpec((tm, d), lambda i: (i, 0))
    vec = pl.BlockSpec((1, d), lambda i: (0, 0))
    return pl.pallas_call(
        _oproj_ln_kernel,
        grid=(tp // tm,),
        in_specs=[row, pl.BlockSpec((d, d), lambda i: (0, 0)), row, vec, vec],
        out_specs=[row, row, pl.BlockSpec((tm * ROW_TILES, LANES), lambda i: (i, 0))],
        out_shape=[jax.ShapeDtypeStruct((tp, d), F32), jax.ShapeDtypeStruct((tp, d), BF16),
                   jax.ShapeDtypeStruct((tp * ROW_TILES, LANES), F32)],
        compiler_params=_cparams(("parallel",), 48),
        name="oproj_ln",
    )(merged, w_o16, h, g.reshape(1, d), b.reshape(1, d))


def _router_kernel(h_ref, w_ref, b_ref, idx_ref, wts_ref, rank_ref, cnt_ref, base_s):
    i = pl.program_id(0)
    tm = h_ref.shape[0]

    @pl.when(i == 0)
    def _():
        base_s[...] = jnp.zeros_like(base_s)

    xh, xl = _split_bf16(h_ref[...], 2)
    wh, wl = _split_bf16(w_ref[...], 2)
    dot = functools.partial(jnp.dot, preferred_element_type=F32)
    scores = jax.nn.sigmoid(dot(xh, wh) + dot(xh, wl) + dot(xl, wh))
    lane = lax.broadcasted_iota(I32, (tm, LANES), 1).astype(F32)
    sel = jnp.where(lane < N_EXPERTS, scores + b_ref[...], NEG_INF)
    chosen = jnp.zeros((tm, LANES), F32)
    idx_out = jnp.zeros((tm, LANES), F32)
    sc_out = jnp.zeros((tm, LANES), F32)
    picks = []
    for k in range(TOP_K):
        m = jnp.max(sel, -1, keepdims=True)
        ik = jnp.min(jnp.where(sel == m, lane, float(LANES)), -1, keepdims=True)
        hit = lane == ik
        sk = jnp.sum(jnp.where(hit, scores, 0.0), -1, keepdims=True)
        sel = jnp.where(hit, NEG_INF, sel)
        chosen = jnp.where(hit, 1.0, chosen)
        idx_out = jnp.where(lane == k, ik, idx_out)
        sc_out = jnp.where(lane == k, sk, sc_out)
        picks.append(hit)
    total = jnp.sum(sc_out, -1, keepdims=True)
    wts_ref[...] = sc_out / total * ROUTED_SCALE
    idx_ref[...] = idx_out.astype(I32)
    before = (lax.broadcasted_iota(I32, (tm, tm), 0) > lax.broadcasted_iota(I32, (tm, tm), 1)).astype(BF16)
    count_before = dot(before, chosen.astype(BF16)) + base_s[0:1, :]
    rank_out = jnp.zeros((tm, LANES), F32)
    for k in range(TOP_K):
        rk = jnp.sum(jnp.where(picks[k], count_before, 0.0), -1, keepdims=True)
        rank_out = jnp.where(lane == k, rk, rank_out)
    rank_ref[...] = rank_out.astype(I32)
    base_s[...] = base_s[...] + jnp.sum(chosen, 0, keepdims=True)
    cnt_ref[...] = base_s[...].astype(I32)


def _router(h, router_w, router_bias, tm=ROW_BLOCK):
    tp, d = h.shape
    w_pad = jnp.zeros((d, LANES), F32).at[:, :N_EXPERTS].set(router_w)
    b_pad = jnp.zeros((1, LANES), F32).at[0, :N_EXPERTS].set(router_bias)
    tok = pl.BlockSpec((tm, LANES), lambda i: (i, 0))
    return pl.pallas_call(
        _router_kernel,
        grid=(tp // tm,),
        in_specs=[pl.BlockSpec((tm, d), lambda i: (i, 0)),
                  pl.BlockSpec((d, LANES), lambda i: (0, 0)),
                  pl.BlockSpec((1, LANES), lambda i: (0, 0))],
        out_specs=[tok, tok, tok, pl.BlockSpec((8, LANES), lambda i: (0, 0))],
        out_shape=[jax.ShapeDtypeStruct((tp, LANES), I32), jax.ShapeDtypeStruct((tp, LANES), F32),
                   jax.ShapeDtypeStruct((tp, LANES), I32), jax.ShapeDtypeStruct((8, LANES), I32)],
        scratch_shapes=[pltpu.VMEM((8, LANES), F32)],
        compiler_params=_cparams(("arbitrary",)),
        name="moe_router",
    )(h, w_pad, b_pad)


MOE_ROWS = 256
DMA_UNROLL = 8


def _row_span(r):
    return pl.ds(pl.multiple_of(r * ROW_TILES, ROW_TILES), ROW_TILES)


def _ffn_kernel(be_ref, tok0_ref, tokn_ref, dstp_ref, dstl_ref, h_hbm, wg_ref, wu_ref, wd_ref, y_hbm,
                x0, x1, y0, y1, gsem, ssem, wg_s, wu_s, wd_s):
    i = pl.program_id(0)
    nb = pl.num_programs(0)
    even = i % 2 == 0
    whole = pl.ds(0, MOE_ROWS * ROW_TILES)

    def gather_row(tok_ref, r, x_next, sem):
        pltpu.make_async_copy(h_hbm.at[_row_span(tok_ref[0, 0, r]), :], x_next.at[_row_span(r), :], sem).start()

    def scatter_row(dst_ref, r, y_from, sem):
        pltpu.make_async_copy(y_from.at[_row_span(r), :], y_hbm.at[_row_span(dst_ref[0, 0, r]), :], sem).start()

    def rolled(fn):
        def body(g, carry):
            for u in range(DMA_UNROLL):
                fn(g * DMA_UNROLL + u)
            return carry

        lax.fori_loop(0, MOE_ROWS // DMA_UNROLL, body, 0)

    def gather_wait(x_buf, sem):
        pltpu.make_async_copy(h_hbm.at[whole, :], x_buf, sem).wait()

    def scatter_wait(y_buf, sem):
        pltpu.make_async_copy(y_buf, y_hbm.at[whole, :], sem).wait()

    @pl.when(i == 0)
    def _():
        y1[...] = jnp.zeros_like(y1)
        rolled(lambda r: gather_row(tok0_ref, r, x0, gsem.at[0]))

    prev = be_ref[jnp.maximum(i - 1, 0)]

    @pl.when((i == 0) | (be_ref[i] != prev))
    def _():
        wg_s[...] = wg_ref[...].astype(BF16)
        wu_s[...] = wu_ref[...].astype(BF16)
        wd_s[...] = wd_ref[...].astype(BF16)

    def step(x_cur, x_next, y_cur, y_prev, cur, nxt):
        gather_wait(x_cur, gsem.at[cur])

        @pl.when(i >= 1)
        def _():
            scatter_wait(y_cur, ssem.at[cur])

        for r in range(MOE_ROWS):
            scatter_row(dstp_ref, r, y_prev, ssem.at[nxt])
        for r in range(MOE_ROWS):
            gather_row(tokn_ref, r, x_next, gsem.at[nxt])
        x = _load_row_linear(x_cur, 0, MOE_ROWS).astype(BF16)
        hid = _silu(jnp.dot(x, wg_s[...], preferred_element_type=F32)) * jnp.dot(
            x, wu_s[...], preferred_element_type=F32)
        _store_row_linear(y_cur, jnp.dot(hid.astype(BF16), wd_s[...], preferred_element_type=F32))

        @pl.when(i == nb - 1)
        def _():
            gather_wait(x_next, gsem.at[nxt])
            scatter_wait(y_prev, ssem.at[nxt])
            rolled(lambda r: scatter_row(dstl_ref, r, y_cur, ssem.at[cur]))
            scatter_wait(y_cur, ssem.at[cur])

    @pl.when(even)
    def _():
        step(x0, x1, y0, y1, 0, 1)

    @pl.when(jnp.logical_not(even))
    def _():
        step(x1, x0, y1, y0, 1, 0)


def _expert_ffn(h_lin, row_tok, row_dst, block_e, e_gate, e_up, e_down, l):
    d = D_MODEL
    tokens = h_lin.shape[0] // ROW_TILES
    nb = block_e.shape[0]
    tok3 = row_tok.reshape(nb, 1, MOE_ROWS)
    spill = TOP_K * tokens + jnp.arange(MOE_ROWS, dtype=I32)
    dst3 = jnp.concatenate([spill, row_dst]).reshape(nb + 1, 1, MOE_ROWS)
    smem_blk = lambda f: pl.BlockSpec((1, 1, MOE_ROWS), f, memory_space=pltpu.SMEM)
    buf = pltpu.VMEM((MOE_ROWS * ROW_TILES, LANES), F32)
    grid_spec = pltpu.PrefetchScalarGridSpec(
        num_scalar_prefetch=1,
        grid=(nb,),
        in_specs=[smem_blk(lambda i, be: (0, 0, 0)),
                  smem_blk(lambda i, be: (jnp.minimum(i + 1, nb - 1), 0, 0)),
                  smem_blk(lambda i, be: (i, 0, 0)),
                  smem_blk(lambda i, be: (nb, 0, 0)),
                  pl.BlockSpec(memory_space=pl.ANY),
                  pl.BlockSpec((None, None, d, D_EXPERT), lambda i, be: (l, be[i], 0, 0)),
                  pl.BlockSpec((None, None, d, D_EXPERT), lambda i, be: (l, be[i], 0, 0)),
                  pl.BlockSpec((None, None, D_EXPERT, d), lambda i, be: (l, be[i], 0, 0))],
        out_specs=pl.BlockSpec(memory_space=pl.ANY),
        scratch_shapes=[buf, buf, buf, buf,
                        pltpu.SemaphoreType.DMA((2,)),
                        pltpu.SemaphoreType.DMA((2,)),
                        pltpu.VMEM((d, D_EXPERT), BF16),
                        pltpu.VMEM((d, D_EXPERT), BF16),
                        pltpu.VMEM((D_EXPERT, d), BF16)],
    )
    return pl.pallas_call(
        _ffn_kernel,
        grid_spec=grid_spec,
        out_shape=jax.ShapeDtypeStruct(((TOP_K * tokens + MOE_ROWS) * ROW_TILES, LANES), F32),
        compiler_params=_cparams(("arbitrary",), 52),
        name="expert_ffn",
    )(block_e, tok3, tok3, dst3, dst3, h_lin, e_gate, e_up, e_down)


def _combine_kernel(*refs, final):
    y_refs = refs[:TOP_K]
    h_ref, h16_ref, wts_ref, sg_ref, su_ref, sd_ref, g_ref, b_ref = refs[TOP_K:TOP_K + 8]
    out_refs = refs[TOP_K + 8:]
    tm = h_ref.shape[0]
    x = h16_ref[...]
    hid = _silu(jnp.dot(x, sg_ref[...], preferred_element_type=F32)) * jnp.dot(
        x, su_ref[...], preferred_element_type=F32)
    acc = jnp.dot(hid.astype(BF16), sd_ref[...], preferred_element_type=F32)
    wts = wts_ref[...]
    for k in range(TOP_K):
        acc = acc + _load_row_linear(y_refs[k], 0, tm) * wts[:, k:k + 1]
    y = _ln_rows(DEEPNORM_ALPHA * h_ref[...] + acc, g_ref[...], b_ref[...])
    out_refs[0][...] = y
    if not final:
        out_refs[1][...] = y.astype(BF16)


def _combine(y_lin, h, h16, wts, sg16, su16, sd16, g, b, lp, final):
    tp, d = h.shape
    tm = ROW_BLOCK
    nt = tp // tm
    bps = lp // tm
    row = pl.BlockSpec((tm, d), lambda i: (i, 0))
    vec = pl.BlockSpec((1, d), lambda i: (0, 0))
    y_specs = [pl.BlockSpec((tm * ROW_TILES, LANES), lambda i, k=k: (k * nt + i, 0)) for k in range(TOP_K)]
    if final:
        out_specs = [pl.BlockSpec((tm, d), lambda i: (_frame_block(i, bps), 0))]
        out_shape = [jax.ShapeDtypeStruct((tp // lp * (lp - tm), d), F32)]
    else:
        out_specs = [row, row]
        out_shape = [jax.ShapeDtypeStruct((tp, d), F32), jax.ShapeDtypeStruct((tp, d), BF16)]
    return pl.pallas_call(
        functools.partial(_combine_kernel, final=final),
        grid=(nt,),
        in_specs=y_specs + [row, row,
                            pl.BlockSpec((tm, LANES), lambda i: (i, 0)),
                            pl.BlockSpec((d, D_EXPERT), lambda i: (0, 0)),
                            pl.BlockSpec((d, D_EXPERT), lambda i: (0, 0)),
                            pl.BlockSpec((D_EXPERT, d), lambda i: (0, 0)),
                            vec, vec],
        out_specs=out_specs,
        out_shape=out_shape,
        compiler_params=_cparams(("arbitrary",), 56),
        name="moe_combine",
    )(*([y_lin] * TOP_K), h, h16, wts, sg16, su16, sd16, g.reshape(1, d), b.reshape(1, d))


def _routing_tables(idx, rank, counts):
    tp = idx.shape[0]
    n_blocks = -(-(tp * TOP_K) // MOE_ROWS) + N_EXPERTS
    n_rows = n_blocks * MOE_ROWS
    padded = (counts + MOE_ROWS - 1) // MOE_ROWS * MOE_ROWS
    pad_end = jnp.cumsum(padded)
    pad_start = pad_end - padded
    dest = pad_start[idx] + rank
    info = jnp.arange(tp, dtype=I32)[:, None] * TOP_K + jnp.arange(TOP_K, dtype=I32)[None, :] + 1
    row_info = jnp.zeros((n_rows,), I32).at[dest.reshape(-1)].set(info.reshape(-1), unique_indices=True)
    filled = row_info > 0
    tok = (row_info - 1) >> 3
    k = (row_info - 1) & (TOP_K - 1)
    row_tok = jnp.where(filled, tok, 0)
    row_dst = jnp.where(filled, k * tp + tok, TOP_K * tp + jnp.arange(n_rows, dtype=I32) % MOE_ROWS)
    block_start = jnp.arange(n_blocks, dtype=I32) * MOE_ROWS
    block_e = jnp.minimum(jnp.sum((pad_end[None, :] <= block_start[:, None]).astype(I32), axis=1), N_EXPERTS - 1)
    return row_tok, row_dst, block_e


def kernel(x, meta, ln_in_g, ln_in_b, w_in, pool_w, pool_scale, conv_w, A_log, dt_bias, gdn_norm_g, w_branch, w_gate, b_gate, w_o, ln1_g, ln1_b, router_w, router_bias, exp_w_gate, exp_w_up, exp_w_down, sh_w_gate, sh_w_up, sh_w_down, ln2_g, ln2_b):
    batch, seq, d = x.shape
    depth = w_in.shape[0]
    lp = PAD_ROWS + N_META + seq
    assert seq % ROW_BLOCK == 0 and d == D_MODEL
    tp = batch * lp
    h, h16 = _ln_in(x.reshape(batch * seq, d), meta, ln_in_g, ln_in_b, lp)
    b_gate3 = b_gate.reshape(depth, 1, N_BRANCH * d)
    tm = 512 if tp % 512 == 0 else ROW_BLOCK

    for l in range(depth):
        w_l = w_in[l]
        w_side = jnp.concatenate([w_l[:, N_MAIN + N_GATES:], w_l[:, N_MAIN:N_MAIN + N_GATES],
                                  jnp.zeros((d, LANES - N_GATES), F32)], axis=1)
        proj_main = _matmul(h16, w_in, pl.BlockSpec((None, d, 512), lambda j, i: (l, 0, j)),
                            N_MAIN, 512, tm, name="proj_main")
        proj_side = _matmul(h16, w_side, pl.BlockSpec((d, 640), lambda j, i: (0, j)),
                            N_SIDE, 640, tm, name="proj_side")
        y_a = _pool_mixer(proj_main, pool_w[l], pool_scale[l], batch, lp)
        qkv = _gdn_conv(proj_main, conv_w[l], batch, lp)
        y_b = _gdn(qkv, proj_main, proj_side, A_log[l], dt_bias[l], gdn_norm_g[l], batch, lp)
        y_c = _stick_breaking(proj_side, batch, lp)
        merged = _merge(h16, y_a, y_b, y_c, w_gate, b_gate3, w_branch, l, tm=tm)
        h, h16, h_lin = _oproj_ln(merged, w_o[l].astype(BF16), h, ln1_g[l], ln1_b[l])

        idx, wts, rank, cnt = _router(h, router_w[l], router_bias[l])
        row_tok, row_dst, block_e = _routing_tables(idx[:, :TOP_K], rank[:, :TOP_K], cnt[0, :N_EXPERTS])
        y_lin = _expert_ffn(h_lin, row_tok, row_dst, block_e, exp_w_gate, exp_w_up, exp_w_down, l)
        out = _combine(y_lin, h, h16, wts, sh_w_gate[l].astype(BF16), sh_w_up[l].astype(BF16),
                       sh_w_down[l].astype(BF16), ln2_g[l], ln2_b[l], lp, final=(l == depth - 1))
        if l < depth - 1:
            h, h16 = out

    return out[0].reshape(batch, seq, d)
```

```python
import functools
import math

import jax
import jax.numpy as jnp
from jax import lax
from jax.experimental import pallas as pl
from jax.experimental.pallas import tpu as pltpu

F32 = jnp.float32
BF16 = jnp.bfloat16
I32 = jnp.int32

D_MODEL = 2048
DEPTH = 4
N_META = 16
ROW_BLOCK = 128
PAD_ROWS = ROW_BLOCK - N_META
POOL_GROUP = 256
N_POOL_GROUPS = 4
D_POOL = POOL_GROUP * N_POOL_GROUPS
HEADS = 8
HEAD_DIM = 128
D_HEADS = HEADS * HEAD_DIM
GDN_CONV = 4
GDN_CHUNK = 64
N_BRANCH = 3
N_EXPERTS = 64
TOP_K = 8
D_EXPERT = 384
ROUTED_SCALE = 2.5
DEEPNORM_ALPHA = (2 * DEPTH) ** 0.25
LN_EPS = 1e-5
RMS_EPS = 1e-6
LANES = 128
N_MAIN = D_POOL + 4 * D_HEADS
N_GATES = 2 * HEADS
N_SB = 3 * D_HEADS
N_SIDE = N_SB + LANES
NEG_INF = float("-inf")


def _cparams(semantics, vmem_mib=None):
    kw = {}
    if vmem_mib is not None:
        kw["vmem_limit_bytes"] = vmem_mib * 2**20
    return pltpu.CompilerParams(dimension_semantics=semantics, **kw)


def _bdot(a, b):
    return jnp.dot(a.astype(BF16), b.astype(BF16), preferred_element_type=F32)


def _bdot_nt(a, b):
    return lax.dot_general(a.astype(BF16), b.astype(BF16), (((1,), (1,)), ((), ())),
                           preferred_element_type=F32)


def _split_bf16(x, parts):
    out, r = [], x
    for _ in range(parts):
        p = r.astype(BF16)
        out.append(p)
        r = r - p.astype(F32)
    return out


def _ln_rows(x, g, b):
    mu = jnp.mean(x, -1, keepdims=True)
    xc = x - mu
    var = jnp.mean(xc * xc, -1, keepdims=True)
    return xc * lax.rsqrt(var + LN_EPS) * g + b


def _silu(x):
    return x * jax.nn.sigmoid(x)


def _softplus(x):
    return jnp.maximum(x, 0.0) + jnp.log1p(jnp.exp(-jnp.abs(x)))


def _ln_in_kernel(x_ref, meta_ref, g_ref, b_ref, h_ref, h16_ref, *, blocks_per_seq):
    first = pl.program_id(0) % blocks_per_seq == 0
    d = x_ref.shape[1]
    lead = jnp.concatenate([jnp.zeros((PAD_ROWS, d), F32), meta_ref[...]], axis=0)
    y = _ln_rows(jnp.where(first, lead, x_ref[...]), g_ref[...], b_ref[...])
    h_ref[...] = y
    h16_ref[...] = y.astype(BF16)


def _frame_block(i, blocks_per_seq):
    b = i // blocks_per_seq
    j = i - b * blocks_per_seq
    return b * (blocks_per_seq - 1) + jnp.maximum(j - 1, 0)


def _ln_in(x2, meta, g, b, lp):
    n, d = x2.shape
    tm = ROW_BLOCK
    bps = lp // tm
    tp = n // (bps - 1) // tm * lp
    row = pl.BlockSpec((tm, d), lambda i: (i, 0))
    vec = pl.BlockSpec((1, d), lambda i: (0, 0))
    return pl.pallas_call(
        functools.partial(_ln_in_kernel, blocks_per_seq=bps),
        grid=(tp // tm,),
        in_specs=[pl.BlockSpec((tm, d), lambda i: (_frame_block(i, bps), 0)),
                  pl.BlockSpec((N_META, d), lambda i: (0, 0)), vec, vec],
        out_specs=[row, row],
        out_shape=[jax.ShapeDtypeStruct((tp, d), F32), jax.ShapeDtypeStruct((tp, d), BF16)],
        compiler_params=_cparams(("arbitrary",)),
        name="ln_in",
    )(x2, meta, g.reshape(1, d), b.reshape(1, d))


def _mm_kernel(x_ref, w_ref, o_ref, wb_ref):
    @pl.when(pl.program_id(1) == 0)
    def _():
        wb_ref[...] = w_ref[...].astype(BF16)

    o_ref[...] = jnp.dot(x_ref[...], wb_ref[...], preferred_element_type=F32).astype(o_ref.dtype)


def _matmul(x16, w, w_spec, n_out, tn, tm=512, name="mm"):
    m, k = x16.shape
    return pl.pallas_call(
        _mm_kernel,
        grid=(n_out // tn, m // tm),
        in_specs=[pl.BlockSpec((tm, k), lambda j, i: (i, 0)), w_spec],
        out_specs=pl.BlockSpec((tm, tn), lambda j, i: (i, j)),
        out_shape=jax.ShapeDtypeStruct((m, n_out), F32),
        scratch_shapes=[pltpu.VMEM((k, tn), BF16)],
        compiler_params=_cparams(("arbitrary", "arbitrary"), 48),
        name=name,
    )(x16, w)


def _pool_kernel(u_ref, w_ref, s_ref, o_ref):
    g = pl.program_id(1)
    lp = u_ref.shape[0]
    t = lax.broadcasted_iota(I32, (lp, 1), 0) - PAD_ROWS
    u = jnp.where(t >= 0, u_ref[...], 0.0)
    s2 = u + pltpu.roll(u, 1, 0)
    s4 = s2 + pltpu.roll(s2, 2, 0)
    s8 = s4 + pltpu.roll(s4, 4, 0)
    s16 = s8 + pltpu.roll(s8, 8, 0)
    s = jnp.where(g == 0, s2, jnp.where(g == 1, s4, jnp.where(g == 2, s8, s16)))
    window = jnp.left_shift(2, g)
    cnt = jnp.maximum(jnp.minimum(t + 1, window), 1).astype(F32)
    pooled = s / cnt - u
    y = _bdot(pooled, w_ref[...]) * s_ref[...]
    o_ref[...] = y.astype(o_ref.dtype)


def _pool_mixer(proj_main, pool_w, pool_scale, batch, lp):
    tp = proj_main.shape[0]
    return pl.pallas_call(
        _pool_kernel,
        grid=(batch, N_POOL_GROUPS),
        in_specs=[pl.BlockSpec((lp, POOL_GROUP), lambda b, g: (b, g)),
                  pl.BlockSpec((None, POOL_GROUP, POOL_GROUP), lambda b, g: (g, 0, 0)),
                  pl.BlockSpec((1, POOL_GROUP), lambda b, g: (0, g))],
        out_specs=pl.BlockSpec((lp, POOL_GROUP), lambda b, g: (b, g)),
        out_shape=jax.ShapeDtypeStruct((tp, D_POOL), BF16),
        compiler_params=_cparams(("parallel", "parallel"), 48),
        name="pool_mixer",
    )(proj_main, pool_w, pool_scale.reshape(1, D_POOL))


def _gdn_conv_kernel(u_ref, w_ref, o_ref):
    c = pl.program_id(1)
    lp = u_ref.shape[0]
    t = lax.broadcasted_iota(I32, (lp, 1), 0) - PAD_ROWS
    u = jnp.where(t >= 0, u_ref[...], 0.0)
    w = w_ref[...]
    y = (u * w[3:4] + pltpu.roll(u, 1, 0) * w[2:3] + pltpu.roll(u, 2, 0) * w[1:2]
         + pltpu.roll(u, 3, 0) * w[0:1])
    y = _silu(y)
    nrm = y * lax.rsqrt(jnp.sum(y * y, -1, keepdims=True) + RMS_EPS)
    o_ref[...] = jnp.where(c < HEADS, nrm * (HEAD_DIM ** -0.5), jnp.where(c < 2 * HEADS, nrm, y))


def _gdn_conv(proj_main, conv_w, batch, lp):
    tp = proj_main.shape[0]
    first = D_POOL // HEAD_DIM
    return pl.pallas_call(
        _gdn_conv_kernel,
        grid=(batch, 3 * HEADS),
        in_specs=[pl.BlockSpec((lp, HEAD_DIM), lambda b, c: (b, first + c)),
                  pl.BlockSpec((GDN_CONV, HEAD_DIM), lambda b, c: (0, c))],
        out_specs=pl.BlockSpec((lp, HEAD_DIM), lambda b, c: (b, c)),
        out_shape=jax.ShapeDtypeStruct((tp, 3 * D_HEADS), F32),
        compiler_params=_cparams(("parallel", "parallel"), 48),
        name="gdn_conv",
    )(proj_main, conv_w)


def _gdn_chunk_kernel(q_ref, k_ref, v_ref, z_ref, ab_ref, alog_ref, dtb_ref, ng_ref, o_ref, s_ref):
    c = pl.program_id(1)
    C = GDN_CHUNK

    @pl.when(c == 0)
    def _():
        s_ref[...] = jnp.zeros_like(s_ref)

    ab = ab_ref[...]
    pos = c * C + lax.broadcasted_iota(I32, (C, 1), 0)
    real = pos >= PAD_ROWS
    g_all = jnp.where(real, -jnp.exp(alog_ref[...]) * _softplus(ab + dtb_ref[...]), 0.0)
    beta_all = jnp.where(real, jax.nn.sigmoid(ab), 0.0)
    ri = lax.broadcasted_iota(I32, (C, C), 0)
    ci = lax.broadcasted_iota(I32, (C, C), 1)
    causal = ri >= ci
    strict = ri > ci
    tri = causal.astype(BF16)
    eye = (ri == ci).astype(F32)
    gcum = sum(jnp.dot(tri, p, preferred_element_type=F32) for p in _split_bf16(g_all, 3))
    gcum_t = jnp.concatenate([gcum, jnp.zeros_like(gcum)], axis=0).T
    ng = ng_ref[...]
    heads = range(HEADS)
    sls = [slice(h * HEAD_DIM, (h + 1) * HEAD_DIM) for h in heads]

    gc_col = [gcum[:, h:h + 1] for h in heads]
    g_last = [gcum[C - 1:C, h:h + 1] for h in heads]
    beta = [beta_all[:, HEADS + h:HEADS + h + 1] for h in heads]
    decay = [jnp.where(causal, jnp.exp(jnp.where(causal, gc_col[h] - gcum_t[h:h + 1, 0:C], 0.0)), 0.0)
             for h in heads]
    eg = [jnp.exp(gc_col[h]) for h in heads]
    qb = [q_ref[:, sls[h]].astype(BF16) for h in heads]
    kf = [k_ref[:, sls[h]] for h in heads]
    kb = [kf[h].astype(BF16) for h in heads]
    kbeta = [kf[h] * beta[h] for h in heads]
    lmat = [jnp.where(strict, _bdot_nt(kbeta[h], kb[h]) * decay[h], 0.0) for h in heads]
    attn = [_bdot_nt(qb[h], kb[h]) * decay[h] for h in heads]
    tinv = [eye - lmat[h] for h in heads]
    pw = [_bdot(lmat[h], lmat[h]) for h in heads]
    for it in range(5):
        tinv = [tinv[h] + _bdot(tinv[h], pw[h]) for h in heads]
        if it < 4:
            pw = [_bdot(pw[h], pw[h]) for h in heads]
    sol = [_bdot(tinv[h], jnp.concatenate([v_ref[:, sls[h]] * beta[h], kbeta[h] * eg[h]], axis=-1))
           for h in heads]
    state = [s_ref[h] for h in heads]
    ws = [_bdot(jnp.concatenate([sol[h][:, HEAD_DIM:], q_ref[:, sls[h]] * eg[h]], axis=0), state[h])
          for h in heads]
    v_new = [sol[h][:, :HEAD_DIM] - ws[h][:C] for h in heads]
    k_dec_t = [(kf[h] * jnp.exp(g_last[h] - gc_col[h])).T for h in heads]
    out = [ws[h][C:] + _bdot(attn[h], v_new[h]) for h in heads]
    for h in heads:
        s_ref[h] = state[h] * jnp.exp(g_last[h]) + _bdot(k_dec_t[h], v_new[h])
    for h in heads:
        o = out[h]
        o = o * lax.rsqrt(jnp.mean(o * o, -1, keepdims=True) + RMS_EPS) * ng
        o_ref[:, sls[h]] = (o * _silu(z_ref[:, sls[h]])).astype(o_ref.dtype)


def _gdn(qkv, proj_main, proj_side, a_log, dt_bias, norm_g, batch, lp):
    tp = qkv.shape[0]
    nc = lp // GDN_CHUNK
    zcol = (D_POOL + 3 * D_HEADS) // D_HEADS
    alog_pad = jnp.zeros((1, LANES), F32).at[0, :HEADS].set(a_log)
    dtb_pad = jnp.zeros((1, LANES), F32).at[0, :HEADS].set(dt_bias)
    row = lambda b, c: b * nc + c
    return pl.pallas_call(
        _gdn_chunk_kernel,
        grid=(batch, nc),
        in_specs=[pl.BlockSpec((GDN_CHUNK, D_HEADS), lambda b, c: (row(b, c), 0)),
                  pl.BlockSpec((GDN_CHUNK, D_HEADS), lambda b, c: (row(b, c), 1)),
                  pl.BlockSpec((GDN_CHUNK, D_HEADS), lambda b, c: (row(b, c), 2)),
                  pl.BlockSpec((GDN_CHUNK, D_HEADS), lambda b, c: (row(b, c), zcol)),
                  pl.BlockSpec((GDN_CHUNK, LANES), lambda b, c: (row(b, c), N_SB // LANES)),
                  pl.BlockSpec((1, LANES), lambda b, c: (0, 0)),
                  pl.BlockSpec((1, LANES), lambda b, c: (0, 0)),
                  pl.BlockSpec((1, HEAD_DIM), lambda b, c: (0, 0))],
        out_specs=pl.BlockSpec((GDN_CHUNK, D_HEADS), lambda b, c: (row(b, c), 0)),
        out_shape=jax.ShapeDtypeStruct((tp, D_HEADS), BF16),
        scratch_shapes=[pltpu.VMEM((HEADS, HEAD_DIM, HEAD_DIM), F32)],
        compiler_params=_cparams(("parallel", "arbitrary"), 48),
        name="gdn_chunks",
    )(qkv, qkv, qkv, proj_main, proj_side, alog_pad, dtb_pad, norm_g.reshape(1, HEAD_DIM))


SB_HEADS_PER_STEP = 8


def _sb_kernel(q_ref, k_ref, v_ref, o_ref, kb_ref, vb_ref, acc_ref):
    i = pl.program_id(2)
    T = ROW_BLOCK
    lp = k_ref.shape[0]
    heads = range(SB_HEADS_PER_STEP)
    sls = [slice(h * HEAD_DIM, (h + 1) * HEAD_DIM) for h in heads]

    @pl.when(i == 0)
    def _():
        real = lax.broadcasted_iota(I32, (lp, 1), 0) >= PAD_ROWS
        kb_ref[...] = k_ref[...].astype(BF16)
        vb_ref[...] = jnp.where(real, v_ref[...], 0.0).astype(BF16)

    q = [(q_ref[:, sls[h]] * (HEAD_DIM ** -0.5)).astype(BF16) for h in heads]
    rowi = lax.broadcasted_iota(I32, (T, T), 0)
    coli = lax.broadcasted_iota(I32, (T, T), 1)
    earlier = coli < rowi
    suffix = (rowi > coli).astype(BF16)

    def key_block(j, later, diagonal):
        start = pl.multiple_of(j * T, T)
        z = [_bdot_nt(q[h], kb_ref[pl.ds(start, T), sls[h]]) for h in heads]
        sp = [_softplus(z[h]) for h in heads]
        lom = [jnp.where(earlier, -sp[h], 0.0) if diagonal else -sp[h] for h in heads]
        parts = [_split_bf16(lom[h], 2) for h in heads]
        between = [later[h] + jnp.dot(parts[h][0], suffix, preferred_element_type=F32)
                   + jnp.dot(parts[h][1], suffix, preferred_element_type=F32) for h in heads]
        attn = [jnp.exp(z[h] - sp[h] + between[h]) for h in heads]
        if diagonal:
            attn = [jnp.where(earlier, attn[h], 0.0) for h in heads]
        pv = [_bdot(attn[h], vb_ref[pl.ds(start, T), sls[h]]) for h in heads]
        for h in heads:
            if diagonal:
                acc_ref[h] = pv[h]
            else:
                acc_ref[h] += pv[h]
        return tuple(later[h] + jnp.sum(lom[h], -1, keepdims=True) for h in heads)

    later = key_block(i, tuple(jnp.zeros((T, 1), F32) for _ in heads), True)
    lax.fori_loop(1, i + 1, lambda n, later: key_block(i - n, later, False), later)
    for h in heads:
        o_ref[:, sls[h]] = acc_ref[h].astype(o_ref.dtype)


def _stick_breaking(proj_side, batch, lp):
    tp = proj_side.shape[0]
    nq = lp // ROW_BLOCK
    hb = SB_HEADS_PER_STEP
    w = hb * HEAD_DIM
    ng = HEADS // hb
    return pl.pallas_call(
        _sb_kernel,
        grid=(batch, ng, nq),
        in_specs=[pl.BlockSpec((ROW_BLOCK, w), lambda b, g, i: (b * nq + i, g)),
                  pl.BlockSpec((lp, w), lambda b, g, i: (b, ng + g)),
                  pl.BlockSpec((lp, w), lambda b, g, i: (b, 2 * ng + g))],
        out_specs=pl.BlockSpec((ROW_BLOCK, w), lambda b, g, i: (b * nq + i, g)),
        out_shape=jax.ShapeDtypeStruct((tp, D_HEADS), BF16),
        scratch_shapes=[pltpu.VMEM((lp, w), BF16), pltpu.VMEM((lp, w), BF16),
                        pltpu.VMEM((hb, ROW_BLOCK, HEAD_DIM), F32)],
        compiler_params=_cparams(("parallel", "parallel", "arbitrary"), 58),
        name="stick_breaking",
    )(proj_side, proj_side, proj_side)


def _merge_kernel(h_ref, ya_ref, yb_ref, yc_ref, wg0_ref, wg1_ref, wg2_ref, bg0_ref, bg1_ref, bg2_ref,
                  wb_ref, o_ref, wg_s, wb_s):
    @pl.when(pl.program_id(1) == 0)
    def _():
        wg_s[0] = wg0_ref[...].astype(BF16)
        wg_s[1] = wg1_ref[...].astype(BF16)
        wg_s[2] = wg2_ref[...].astype(BF16)
        wb_s[...] = wb_ref[...].astype(BF16)

    h = h_ref[...]
    acc = None
    for n, (y_ref, bg_ref) in enumerate(((ya_ref, bg0_ref), (yb_ref, bg1_ref), (yc_ref, bg2_ref))):
        gate = jax.nn.sigmoid(jnp.dot(h, wg_s[n], preferred_element_type=F32) + bg_ref[...])
        up = jnp.dot(y_ref[...], wb_s[n], preferred_element_type=F32)
        acc = gate * up if acc is None else acc + gate * up
    o_ref[...] = acc.astype(o_ref.dtype)


def _merge(h16, ya, yb, yc, w_gate, b_gate, w_branch, l, tn=256, tm=512):
    tp = h16.shape[0]
    nj = D_MODEL // tn
    wg_spec = lambda n: pl.BlockSpec((None, D_MODEL, tn), lambda j, i: (l, 0, n * nj + j))
    bg_spec = lambda n: pl.BlockSpec((None, 1, tn), lambda j, i: (l, 0, n * nj + j))
    y_spec = pl.BlockSpec((tm, D_HEADS), lambda j, i: (i, 0))
    return pl.pallas_call(
        _merge_kernel,
        grid=(nj, tp // tm),
        in_specs=[pl.BlockSpec((tm, D_MODEL), lambda j, i: (i, 0)), y_spec, y_spec, y_spec,
                  wg_spec(0), wg_spec(1), wg_spec(2), bg_spec(0), bg_spec(1), bg_spec(2),
                  pl.BlockSpec((None, N_BRANCH, D_HEADS, tn), lambda j, i: (l, 0, 0, j))],
        out_specs=pl.BlockSpec((tm, tn), lambda j, i: (i, j)),
        out_shape=jax.ShapeDtypeStruct((tp, D_MODEL), BF16),
        scratch_shapes=[pltpu.VMEM((N_BRANCH, D_MODEL, tn), BF16),
                        pltpu.VMEM((N_BRANCH, D_HEADS, tn), BF16)],
        compiler_params=_cparams(("arbitrary", "arbitrary"), 56),
        name="branch_merge",
    )(h16, ya, yb, yc, w_gate, w_gate, w_gate, b_gate, b_gate, b_gate, w_branch)


ROW_TILES = D_MODEL // (2 * LANES)
HIGH_HALF = -65536


def _store_rows_packed(ref, x):
    rows = x.shape[0]
    half = D_MODEL // 2
    bits = lax.bitcast_convert_type(x.astype(BF16).astype(F32), I32)
    words = lax.shift_right_logical(bits[:, :half], 16) | bits[:, half:]
    for c in range(ROW_TILES):
        ref[pl.ds(c, rows, stride=ROW_TILES), :] = words[:, c * LANES:(c + 1) * LANES]


def _load_rows_packed(ref, first_row, rows):
    words = [ref[pl.ds(first_row * ROW_TILES + c, rows, stride=ROW_TILES), :] for c in range(ROW_TILES)]
    low = [lax.bitcast_convert_type(lax.shift_left(w, 16), F32) for w in words]
    high = [lax.bitcast_convert_type(w & HIGH_HALF, F32) for w in words]
    return jnp.concatenate(low + high, axis=-1)


def _oproj_ln_kernel(m_ref, w_ref, h_ref, g_ref, b_ref, ho_ref, h16o_ref, hpk_ref):
    mix = jnp.dot(m_ref[...], w_ref[...], preferred_element_type=F32)
    y = _ln_rows(DEEPNORM_ALPHA * h_ref[...] + mix, g_ref[...], b_ref[...])
    ho_ref[...] = y
    h16o_ref[...] = y.astype(BF16)
    _store_rows_packed(hpk_ref, y)


def _oproj_ln(merged, w_o16, h, g, b, tm=256):
    tp, d = h.shape
    row = pl.BlockSpec((tm, d), lambda i: (i, 0))
    vec = pl.BlockSpec((1, d), lambda i: (0, 0))
    return pl.pallas_call(
        _oproj_ln_kernel,
        grid=(tp // tm,),
        in_specs=[row, pl.BlockSpec((d, d), lambda i: (0, 0)), row, vec, vec],
        out_specs=[row, row, pl.BlockSpec((tm * ROW_TILES, LANES), lambda i: (i, 0))],
        out_shape=[jax.ShapeDtypeStruct((tp, d), F32), jax.ShapeDtypeStruct((tp, d), BF16),
                   jax.ShapeDtypeStruct((tp * ROW_TILES, LANES), I32)],
        compiler_params=_cparams(("parallel",), 48),
        name="oproj_ln",
    )(merged, w_o16, h, g.reshape(1, d), b.reshape(1, d))


def _router_kernel(h_ref, w_ref, b_ref, idx_ref, wts_ref, rank_ref, cnt_ref, base_s):
    i = pl.program_id(0)
    tm = h_ref.shape[0]

    @pl.when(i == 0)
    def _():
        base_s[...] = jnp.zeros_like(base_s)

    xh, xl = _split_bf16(h_ref[...], 2)
    wh, wl = _split_bf16(w_ref[...], 2)
    dot = functools.partial(jnp.dot, preferred_element_type=F32)
    scores = jax.nn.sigmoid(dot(xh, wh) + dot(xh, wl) + dot(xl, wh))
    lane = lax.broadcasted_iota(I32, (tm, LANES), 1).astype(F32)
    sel = jnp.where(lane < N_EXPERTS, scores + b_ref[...], NEG_INF)
    chosen = jnp.zeros((tm, LANES), F32)
    idx_out = jnp.zeros((tm, LANES), F32)
    sc_out = jnp.zeros((tm, LANES), F32)
    picks = []
    for k in range(TOP_K):
        m = jnp.max(sel, -1, keepdims=True)
        ik = jnp.min(jnp.where(sel == m, lane, float(LANES)), -1, keepdims=True)
        hit = lane == ik
        sk = jnp.sum(jnp.where(hit, scores, 0.0), -1, keepdims=True)
        sel = jnp.where(hit, NEG_INF, sel)
        chosen = jnp.where(hit, 1.0, chosen)
        idx_out = jnp.where(lane == k, ik, idx_out)
        sc_out = jnp.where(lane == k, sk, sc_out)
        picks.append(hit)
    total = jnp.sum(sc_out, -1, keepdims=True)
    wts_ref[...] = sc_out / total * ROUTED_SCALE
    idx_ref[...] = idx_out.astype(I32)
    before = (lax.broadcasted_iota(I32, (tm, tm), 0) > lax.broadcasted_iota(I32, (tm, tm), 1)).astype(BF16)
    count_before = dot(before, chosen.astype(BF16)) + base_s[0:1, :]
    rank_out = jnp.zeros((tm, LANES), F32)
    for k in range(TOP_K):
        rk = jnp.sum(jnp.where(picks[k], count_before, 0.0), -1, keepdims=True)
        rank_out = jnp.where(lane == k, rk, rank_out)
    rank_ref[...] = rank_out.astype(I32)
    base_s[...] = base_s[...] + jnp.sum(chosen, 0, keepdims=True)
    cnt_ref[...] = base_s[...].astype(I32)


def _router(h, router_w, router_bias, tm=ROW_BLOCK):
    tp, d = h.shape
    w_pad = jnp.zeros((d, LANES), F32).at[:, :N_EXPERTS].set(router_w)
    b_pad = jnp.zeros((1, LANES), F32).at[0, :N_EXPERTS].set(router_bias)
    tok = pl.BlockSpec((tm, LANES), lambda i: (i, 0))
    return pl.pallas_call(
        _router_kernel,
        grid=(tp // tm,),
        in_specs=[pl.BlockSpec((tm, d), lambda i: (i, 0)),
                  pl.BlockSpec((d, LANES), lambda i: (0, 0)),
                  pl.BlockSpec((1, LANES), lambda i: (0, 0))],
        out_specs=[tok, tok, tok, pl.BlockSpec((8, LANES), lambda i: (0, 0))],
        out_shape=[jax.ShapeDtypeStruct((tp, LANES), I32), jax.ShapeDtypeStruct((tp, LANES), F32),
                   jax.ShapeDtypeStruct((tp, LANES), I32), jax.ShapeDtypeStruct((8, LANES), I32)],
        scratch_shapes=[pltpu.VMEM((8, LANES), F32)],
        compiler_params=_cparams(("arbitrary",)),
        name="moe_router",
    )(h, w_pad, b_pad)


MOE_ROWS = 256
DMA_UNROLL = 8


def _row_span(r):
    return pl.ds(pl.multiple_of(r * ROW_TILES, ROW_TILES), ROW_TILES)


def _ffn_kernel(be_ref, nu_ref, tok0_ref, tokn_ref, dstp_ref, dstl_ref, h_hbm, wg_ref, wu_ref, wd_ref, y_hbm,
                x0, x1, y0, y1, gsem, ssem, wg_s, wu_s, wd_s):
    i = pl.program_id(0)
    n_used = nu_ref[0]
    even = i % 2 == 0
    whole = pl.ds(0, MOE_ROWS * ROW_TILES)

    def gather_row(tok_ref, r, x_next, sem):
        pltpu.make_async_copy(h_hbm.at[_row_span(tok_ref[0, 0, r]), :], x_next.at[_row_span(r), :], sem).start()

    def scatter_row(dst_ref, r, y_from, sem):
        pltpu.make_async_copy(y_from.at[_row_span(r), :], y_hbm.at[_row_span(dst_ref[0, 0, r]), :], sem).start()

    def rolled(fn):
        def body(g, carry):
            for u in range(DMA_UNROLL):
                fn(g * DMA_UNROLL + u)
            return carry

        lax.fori_loop(0, MOE_ROWS // DMA_UNROLL, body, 0)

    def gather_wait(x_buf, sem):
        pltpu.make_async_copy(h_hbm.at[whole, :], x_buf, sem).wait()

    def scatter_wait(y_buf, sem):
        pltpu.make_async_copy(y_buf, y_hbm.at[whole, :], sem).wait()

    @pl.when(i == 0)
    def _():
        y1[...] = jnp.zeros_like(y1)
        rolled(lambda r: gather_row(tok0_ref, r, x0, gsem.at[0]))

    prev = be_ref[jnp.maximum(i - 1, 0)]

    @pl.when((i == 0) | (be_ref[i] != prev))
    def _():
        wg_s[...] = wg_ref[...].astype(BF16)
        wu_s[...] = wu_ref[...].astype(BF16)
        wd_s[...] = wd_ref[...].astype(BF16)

    def step(x_cur, x_next, y_cur, y_prev, cur, nxt):
        gather_wait(x_cur, gsem.at[cur])

        @pl.when(i >= 1)
        def _():
            scatter_wait(y_cur, ssem.at[cur])

        for r in range(MOE_ROWS):
            scatter_row(dstp_ref, r, y_prev, ssem.at[nxt])
        for r in range(MOE_ROWS):
            gather_row(tokn_ref, r, x_next, gsem.at[nxt])
        x = _load_rows_packed(x_cur, 0, MOE_ROWS).astype(BF16)
        hid = _silu(jnp.dot(x, wg_s[...], preferred_element_type=F32)) * jnp.dot(
            x, wu_s[...], preferred_element_type=F32)
        _store_rows_packed(y_cur, jnp.dot(hid.astype(BF16), wd_s[...], preferred_element_type=F32))

        @pl.when(i == n_used - 1)
        def _():
            gather_wait(x_next, gsem.at[nxt])
            scatter_wait(y_prev, ssem.at[nxt])
            rolled(lambda r: scatter_row(dstl_ref, r, y_cur, ssem.at[cur]))
            scatter_wait(y_cur, ssem.at[cur])

    @pl.when(even & (i < n_used))
    def _():
        step(x0, x1, y0, y1, 0, 1)

    @pl.when(jnp.logical_not(even) & (i < n_used))
    def _():
        step(x1, x0, y1, y0, 1, 0)


def _expert_ffn(h_pk, row_tok, row_dst, block_e, n_used, e_gate, e_up, e_down, l):
    d = D_MODEL
    tokens = h_pk.shape[0] // ROW_TILES
    nb = block_e.shape[0]
    tok3 = row_tok.reshape(nb, 1, MOE_ROWS)
    spill = TOP_K * tokens + jnp.arange(MOE_ROWS, dtype=I32)
    dst3 = jnp.concatenate([spill, row_dst]).reshape(nb + 1, 1, MOE_ROWS)
    smem_blk = lambda f: pl.BlockSpec((1, 1, MOE_ROWS), f, memory_space=pltpu.SMEM)
    buf = pltpu.VMEM((MOE_ROWS * ROW_TILES, LANES), I32)
    grid_spec = pltpu.PrefetchScalarGridSpec(
        num_scalar_prefetch=2,
        grid=(nb,),
        in_specs=[smem_blk(lambda i, be, nu: (0, 0, 0)),
                  smem_blk(lambda i, be, nu: (jnp.minimum(i + 1, nb - 1), 0, 0)),
                  smem_blk(lambda i, be, nu: (i, 0, 0)),
                  smem_blk(lambda i, be, nu: (nu[0], 0, 0)),
                  pl.BlockSpec(memory_space=pl.ANY),
                  pl.BlockSpec((None, None, d, D_EXPERT), lambda i, be, nu: (l, be[i], 0, 0)),
                  pl.BlockSpec((None, None, d, D_EXPERT), lambda i, be, nu: (l, be[i], 0, 0)),
                  pl.BlockSpec((None, None, D_EXPERT, d), lambda i, be, nu: (l, be[i], 0, 0))],
        out_specs=pl.BlockSpec(memory_space=pl.ANY),
        scratch_shapes=[buf, buf, buf, buf,
                        pltpu.SemaphoreType.DMA((2,)),
                        pltpu.SemaphoreType.DMA((2,)),
                        pltpu.VMEM((d, D_EXPERT), BF16),
                        pltpu.VMEM((d, D_EXPERT), BF16),
                        pltpu.VMEM((D_EXPERT, d), BF16)],
    )
    return pl.pallas_call(
        _ffn_kernel,
        grid_spec=grid_spec,
        out_shape=jax.ShapeDtypeStruct(((TOP_K * tokens + MOE_ROWS) * ROW_TILES, LANES), I32),
        compiler_params=_cparams(("arbitrary",), 52),
        name="expert_ffn",
    )(block_e, n_used, tok3, tok3, dst3, dst3, h_pk, e_gate, e_up, e_down)


def _combine_kernel(*refs, final):
    y_refs = refs[:TOP_K]
    h_ref, h16_ref, wts_ref, sg_ref, su_ref, sd_ref, g_ref, b_ref = refs[TOP_K:TOP_K + 8]
    out_refs = refs[TOP_K + 8:]
    tm = h_ref.shape[0]
    x = h16_ref[...]
    hid = _silu(jnp.dot(x, sg_ref[...], preferred_element_type=F32)) * jnp.dot(
        x, su_ref[...], preferred_element_type=F32)
    acc = jnp.dot(hid.astype(BF16), sd_ref[...], preferred_element_type=F32)
    wts = wts_ref[...]
    for k in range(TOP_K):
        acc = acc + _load_rows_packed(y_refs[k], 0, tm) * wts[:, k:k + 1]
    y = _ln_rows(DEEPNORM_ALPHA * h_ref[...] + acc, g_ref[...], b_ref[...])
    out_refs[0][...] = y
    if not final:
        out_refs[1][...] = y.astype(BF16)


def _combine(y_lin, h, h16, wts, sg16, su16, sd16, g, b, lp, final):
    tp, d = h.shape
    tm = ROW_BLOCK
    nt = tp // tm
    bps = lp // tm
    row = pl.BlockSpec((tm, d), lambda i: (i, 0))
    vec = pl.BlockSpec((1, d), lambda i: (0, 0))
    y_specs = [pl.BlockSpec((tm * ROW_TILES, LANES), lambda i, k=k: (k * nt + i, 0)) for k in range(TOP_K)]
    if final:
        out_specs = [pl.BlockSpec((tm, d), lambda i: (_frame_block(i, bps), 0))]
        out_shape = [jax.ShapeDtypeStruct((tp // lp * (lp - tm), d), F32)]
    else:
        out_specs = [row, row]
        out_shape = [jax.ShapeDtypeStruct((tp, d), F32), jax.ShapeDtypeStruct((tp, d), BF16)]
    return pl.pallas_call(
        functools.partial(_combine_kernel, final=final),
        grid=(nt,),
        in_specs=y_specs + [row, row,
                            pl.BlockSpec((tm, LANES), lambda i: (i, 0)),
                            pl.BlockSpec((d, D_EXPERT), lambda i: (0, 0)),
                            pl.BlockSpec((d, D_EXPERT), lambda i: (0, 0)),
                            pl.BlockSpec((D_EXPERT, d), lambda i: (0, 0)),
                            vec, vec],
        out_specs=out_specs,
        out_shape=out_shape,
        compiler_params=_cparams(("arbitrary",), 56),
        name="moe_combine",
    )(*([y_lin] * TOP_K), h, h16, wts, sg16, su16, sd16, g.reshape(1, d), b.reshape(1, d))


def _routing_tables(idx, rank, counts):
    tp = idx.shape[0]
    n_blocks = -(-(tp * TOP_K) // MOE_ROWS) + N_EXPERTS
    n_rows = n_blocks * MOE_ROWS
    padded = (counts + MOE_ROWS - 1) // MOE_ROWS * MOE_ROWS
    pad_end = jnp.cumsum(padded)
    pad_start = pad_end - padded
    dest = pad_start[idx] + rank
    info = jnp.arange(tp, dtype=I32)[:, None] * TOP_K + jnp.arange(TOP_K, dtype=I32)[None, :] + 1
    row_info = jnp.zeros((n_rows,), I32).at[dest.reshape(-1)].set(info.reshape(-1), unique_indices=True)
    filled = row_info > 0
    tok = (row_info - 1) >> 3
    k = (row_info - 1) & (TOP_K - 1)
    row_tok = jnp.where(filled, tok, 0)
    row_dst = jnp.where(filled, k * tp + tok, TOP_K * tp + jnp.arange(n_rows, dtype=I32) % MOE_ROWS)
    block_start = jnp.arange(n_blocks, dtype=I32) * MOE_ROWS
    block_e = jnp.minimum(jnp.sum((pad_end[None, :] <= block_start[:, None]).astype(I32), axis=1), N_EXPERTS - 1)
    n_used = (pad_end[-1] // MOE_ROWS).astype(I32).reshape(1)
    return row_tok, row_dst, block_e, n_used


def kernel(x, meta, ln_in_g, ln_in_b, w_in, pool_w, pool_scale, conv_w, A_log, dt_bias, gdn_norm_g, w_branch, w_gate, b_gate, w_o, ln1_g, ln1_b, router_w, router_bias, exp_w_gate, exp_w_up, exp_w_down, sh_w_gate, sh_w_up, sh_w_down, ln2_g, ln2_b):
    batch, seq, d = x.shape
    depth = w_in.shape[0]
    lp = PAD_ROWS + N_META + seq
    assert seq % ROW_BLOCK == 0 and d == D_MODEL
    tp = batch * lp
    h, h16 = _ln_in(x.reshape(batch * seq, d), meta, ln_in_g, ln_in_b, lp)
    b_gate3 = b_gate.reshape(depth, 1, N_BRANCH * d)
    tm = 512 if tp % 512 == 0 else ROW_BLOCK

    for l in range(depth):
        w_l = w_in[l]
        w_side = jnp.concatenate([w_l[:, N_MAIN + N_GATES:], w_l[:, N_MAIN:N_MAIN + N_GATES],
                                  jnp.zeros((d, LANES - N_GATES), F32)], axis=1)
        proj_main = _matmul(h16, w_in, pl.BlockSpec((None, d, 512), lambda j, i: (l, 0, j)),
                            N_MAIN, 512, tm, name="proj_main")
        proj_side = _matmul(h16, w_side, pl.BlockSpec((d, 640), lambda j, i: (0, j)),
                            N_SIDE, 640, tm, name="proj_side")
        y_a = _pool_mixer(proj_main, pool_w[l], pool_scale[l], batch, lp)
        qkv = _gdn_conv(proj_main, conv_w[l], batch, lp)
        y_b = _gdn(qkv, proj_main, proj_side, A_log[l], dt_bias[l], gdn_norm_g[l], batch, lp)
        y_c = _stick_breaking(proj_side, batch, lp)
        merged = _merge(h16, y_a, y_b, y_c, w_gate, b_gate3, w_branch, l, tm=tm)
        h, h16, h_pk = _oproj_ln(merged, w_o[l].astype(BF16), h, ln1_g[l], ln1_b[l])

        idx, wts, rank, cnt = _router(h, router_w[l], router_bias[l])
        row_tok, row_dst, block_e, n_used = _routing_tables(idx[:, :TOP_K], rank[:, :TOP_K], cnt[0, :N_EXPERTS])
        y_pk = _expert_ffn(h_pk, row_tok, row_dst, block_e, n_used, exp_w_gate, exp_w_up, exp_w_down, l)
        out = _combine(y_pk, h, h16, wts, sh_w_gate[l].astype(BF16), sh_w_up[l].astype(BF16),
                       sh_w_down[l].astype(BF16), ln2_g[l], ln2_b[l], lp, final=(l == depth - 1))
        if l < depth - 1:
            h, h16 = out

    return out[0].reshape(batch, seq, d)
```

```python
import functools
import math

import jax
import jax.numpy as jnp
from jax import lax
from jax.experimental import pallas as pl
from jax.experimental.pallas import tpu as pltpu

F32 = jnp.float32
BF16 = jnp.bfloat16
I32 = jnp.int32

D_MODEL = 2048
DEPTH = 4
N_META = 16
ROW_BLOCK = 128
PAD_ROWS = ROW_BLOCK - N_META
POOL_GROUP = 256
N_POOL_GROUPS = 4
D_POOL = POOL_GROUP * N_POOL_GROUPS
HEADS = 8
HEAD_DIM = 128
D_HEADS = HEADS * HEAD_DIM
GDN_CONV = 4
GDN_CHUNK = 64
N_BRANCH = 3
N_EXPERTS = 64
TOP_K = 8
D_EXPERT = 384
ROUTED_SCALE = 2.5
DEEPNORM_ALPHA = (2 * DEPTH) ** 0.25
LN_EPS = 1e-5
RMS_EPS = 1e-6
LANES = 128
N_MAIN = D_POOL + 4 * D_HEADS
N_GATES = 2 * HEADS
N_SB = 3 * D_HEADS
N_SIDE = N_SB + LANES
NEG_INF = float("-inf")


def _cparams(semantics, vmem_mib=None):
    kw = {}
    if vmem_mib is not None:
        kw["vmem_limit_bytes"] = vmem_mib * 2**20
    return pltpu.CompilerParams(dimension_semantics=semantics, **kw)


def _bdot(a, b):
    return jnp.dot(a.astype(BF16), b.astype(BF16), preferred_element_type=F32)


def _bdot_nt(a, b):
    return lax.dot_general(a.astype(BF16), b.astype(BF16), (((1,), (1,)), ((), ())),
                           preferred_element_type=F32)


def _split_bf16(x, parts):
    out, r = [], x
    for _ in range(parts):
        p = r.astype(BF16)
        out.append(p)
        r = r - p.astype(F32)
    return out


def _ln_rows(x, g, b):
    mu = jnp.mean(x, -1, keepdims=True)
    xc = x - mu
    var = jnp.mean(xc * xc, -1, keepdims=True)
    return xc * lax.rsqrt(var + LN_EPS) * g + b


def _silu(x):
    return x * jax.nn.sigmoid(x)


def _softplus(x):
    return jnp.maximum(x, 0.0) + jnp.log1p(jnp.exp(-jnp.abs(x)))


def _ln_in_kernel(x_ref, meta_ref, g_ref, b_ref, h_ref, h16_ref, *, blocks_per_seq):
    first = pl.program_id(0) % blocks_per_seq == 0
    d = x_ref.shape[1]
    lead = jnp.concatenate([jnp.zeros((PAD_ROWS, d), F32), meta_ref[...]], axis=0)
    y = _ln_rows(jnp.where(first, lead, x_ref[...]), g_ref[...], b_ref[...])
    h_ref[...] = y
    h16_ref[...] = y.astype(BF16)


def _frame_block(i, blocks_per_seq):
    b = i // blocks_per_seq
    j = i - b * blocks_per_seq
    return b * (blocks_per_seq - 1) + jnp.maximum(j - 1, 0)


def _ln_in(x2, meta, g, b, lp):
    n, d = x2.shape
    tm = ROW_BLOCK
    bps = lp // tm
    tp = n // (bps - 1) // tm * lp
    row = pl.BlockSpec((tm, d), lambda i: (i, 0))
    vec = pl.BlockSpec((1, d), lambda i: (0, 0))
    return pl.pallas_call(
        functools.partial(_ln_in_kernel, blocks_per_seq=bps),
        grid=(tp // tm,),
        in_specs=[pl.BlockSpec((tm, d), lambda i: (_frame_block(i, bps), 0)),
                  pl.BlockSpec((N_META, d), lambda i: (0, 0)), vec, vec],
        out_specs=[row, row],
        out_shape=[jax.ShapeDtypeStruct((tp, d), F32), jax.ShapeDtypeStruct((tp, d), BF16)],
        compiler_params=_cparams(("arbitrary",)),
        name="ln_in",
    )(x2, meta, g.reshape(1, d), b.reshape(1, d))


def _mm_kernel(x_ref, w_ref, o_ref, wb_ref):
    @pl.when(pl.program_id(1) == 0)
    def _():
        wb_ref[...] = w_ref[...].astype(BF16)

    o_ref[...] = jnp.dot(x_ref[...], wb_ref[...], preferred_element_type=F32).astype(o_ref.dtype)


def _matmul(x16, w, w_spec, n_out, tn, tm=512, name="mm"):
    m, k = x16.shape
    return pl.pallas_call(
        _mm_kernel,
        grid=(n_out // tn, m // tm),
        in_specs=[pl.BlockSpec((tm, k), lambda j, i: (i, 0)), w_spec],
        out_specs=pl.BlockSpec((tm, tn), lambda j, i: (i, j)),
        out_shape=jax.ShapeDtypeStruct((m, n_out), F32),
        scratch_shapes=[pltpu.VMEM((k, tn), BF16)],
        compiler_params=_cparams(("arbitrary", "arbitrary"), 48),
        name=name,
    )(x16, w)


def _pool_kernel(u_ref, w_ref, s_ref, o_ref):
    g = pl.program_id(1)
    lp = u_ref.shape[0]
    t = lax.broadcasted_iota(I32, (lp, 1), 0) - PAD_ROWS
    u = jnp.where(t >= 0, u_ref[...], 0.0)
    s2 = u + pltpu.roll(u, 1, 0)
    s4 = s2 + pltpu.roll(s2, 2, 0)
    s8 = s4 + pltpu.roll(s4, 4, 0)
    s16 = s8 + pltpu.roll(s8, 8, 0)
    s = jnp.where(g == 0, s2, jnp.where(g == 1, s4, jnp.where(g == 2, s8, s16)))
    window = jnp.left_shift(2, g)
    cnt = jnp.maximum(jnp.minimum(t + 1, window), 1).astype(F32)
    pooled = s / cnt - u
    y = _bdot(pooled, w_ref[...]) * s_ref[...]
    o_ref[...] = y.astype(o_ref.dtype)


def _pool_mixer(proj_main, pool_w, pool_scale, batch, lp):
    tp = proj_main.shape[0]
    return pl.pallas_call(
        _pool_kernel,
        grid=(batch, N_POOL_GROUPS),
        in_specs=[pl.BlockSpec((lp, POOL_GROUP), lambda b, g: (b, g)),
                  pl.BlockSpec((None, POOL_GROUP, POOL_GROUP), lambda b, g: (g, 0, 0)),
                  pl.BlockSpec((1, POOL_GROUP), lambda b, g: (0, g))],
        out_specs=pl.BlockSpec((lp, POOL_GROUP), lambda b, g: (b, g)),
        out_shape=jax.ShapeDtypeStruct((tp, D_POOL), BF16),
        compiler_params=_cparams(("parallel", "parallel"), 48),
        name="pool_mixer",
    )(proj_main, pool_w, pool_scale.reshape(1, D_POOL))


def _gdn_conv_kernel(u_ref, w_ref, o_ref):
    c = pl.program_id(1)
    lp = u_ref.shape[0]
    t = lax.broadcasted_iota(I32, (lp, 1), 0) - PAD_ROWS
    u = jnp.where(t >= 0, u_ref[...], 0.0)
    w = w_ref[...]
    y = (u * w[3:4] + pltpu.roll(u, 1, 0) * w[2:3] + pltpu.roll(u, 2, 0) * w[1:2]
         + pltpu.roll(u, 3, 0) * w[0:1])
    y = _silu(y)
    nrm = y * lax.rsqrt(jnp.sum(y * y, -1, keepdims=True) + RMS_EPS)
    o_ref[...] = jnp.where(c < HEADS, nrm * (HEAD_DIM ** -0.5), jnp.where(c < 2 * HEADS, nrm, y))


def _gdn_conv(proj_main, conv_w, batch, lp):
    tp = proj_main.shape[0]
    first = D_POOL // HEAD_DIM
    return pl.pallas_call(
        _gdn_conv_kernel,
        grid=(batch, 3 * HEADS),
        in_specs=[pl.BlockSpec((lp, HEAD_DIM), lambda b, c: (b, first + c)),
                  pl.BlockSpec((GDN_CONV, HEAD_DIM), lambda b, c: (0, c))],
        out_specs=pl.BlockSpec((lp, HEAD_DIM), lambda b, c: (b, c)),
        out_shape=jax.ShapeDtypeStruct((tp, 3 * D_HEADS), F32),
        compiler_params=_cparams(("parallel", "parallel"), 48),
        name="gdn_conv",
    )(proj_main, conv_w)


def _gdn_chunk_kernel(q_ref, k_ref, v_ref, z_ref, ab_ref, alog_ref, dtb_ref, ng_ref, o_ref, s_ref):
    c = pl.program_id(1)
    C = GDN_CHUNK

    @pl.when(c == 0)
    def _():
        s_ref[...] = jnp.zeros_like(s_ref)

    ab = ab_ref[...]
    pos = c * C + lax.broadcasted_iota(I32, (C, 1), 0)
    real = pos >= PAD_ROWS
    g_all = jnp.where(real, -jnp.exp(alog_ref[...]) * _softplus(ab + dtb_ref[...]), 0.0)
    beta_all = jnp.where(real, jax.nn.sigmoid(ab), 0.0)
    ri = lax.broadcasted_iota(I32, (C, C), 0)
    ci = lax.broadcasted_iota(I32, (C, C), 1)
    causal = ri >= ci
    strict = ri > ci
    tri = causal.astype(BF16)
    eye = (ri == ci).astype(F32)
    gcum = sum(jnp.dot(tri, p, preferred_element_type=F32) for p in _split_bf16(g_all, 3))
    gcum_t = jnp.concatenate([gcum, jnp.zeros_like(gcum)], axis=0).T
    ng = ng_ref[...]
    heads = range(HEADS)
    sls = [slice(h * HEAD_DIM, (h + 1) * HEAD_DIM) for h in heads]

    gc_col = [gcum[:, h:h + 1] for h in heads]
    g_last = [gcum[C - 1:C, h:h + 1] for h in heads]
    beta = [beta_all[:, HEADS + h:HEADS + h + 1] for h in heads]
    decay = [jnp.where(causal, jnp.exp(jnp.where(causal, gc_col[h] - gcum_t[h:h + 1, 0:C], 0.0)), 0.0)
             for h in heads]
    eg = [jnp.exp(gc_col[h]) for h in heads]
    qb = [q_ref[:, sls[h]].astype(BF16) for h in heads]
    kf = [k_ref[:, sls[h]] for h in heads]
    kb = [kf[h].astype(BF16) for h in heads]
    kbeta = [kf[h] * beta[h] for h in heads]
    lmat = [jnp.where(strict, _bdot_nt(kbeta[h], kb[h]) * decay[h], 0.0) for h in heads]
    attn = [_bdot_nt(qb[h], kb[h]) * decay[h] for h in heads]
    tinv = [eye - lmat[h] for h in heads]
    pw = [_bdot(lmat[h], lmat[h]) for h in heads]
    for it in range(5):
        tinv = [tinv[h] + _bdot(tinv[h], pw[h]) for h in heads]
        if it < 4:
            pw = [_bdot(pw[h], pw[h]) for h in heads]
    sol = [_bdot(tinv[h], jnp.concatenate([v_ref[:, sls[h]] * beta[h], kbeta[h] * eg[h]], axis=-1))
           for h in heads]
    state = [s_ref[h] for h in heads]
    ws = [_bdot(jnp.concatenate([sol[h][:, HEAD_DIM:], q_ref[:, sls[h]] * eg[h]], axis=0), state[h])
          for h in heads]
    v_new = [sol[h][:, :HEAD_DIM] - ws[h][:C] for h in heads]
    k_dec_t = [(kf[h] * jnp.exp(g_last[h] - gc_col[h])).T for h in heads]
    out = [ws[h][C:] + _bdot(attn[h], v_new[h]) for h in heads]
    for h in heads:
        s_ref[h] = state[h] * jnp.exp(g_last[h]) + _bdot(k_dec_t[h], v_new[h])
    for h in heads:
        o = out[h]
        o = o * lax.rsqrt(jnp.mean(o * o, -1, keepdims=True) + RMS_EPS) * ng
        o_ref[:, sls[h]] = (o * _silu(z_ref[:, sls[h]])).astype(o_ref.dtype)


def _gdn(qkv, proj_main, proj_side, a_log, dt_bias, norm_g, batch, lp):
    tp = qkv.shape[0]
    nc = lp // GDN_CHUNK
    zcol = (D_POOL + 3 * D_HEADS) // D_HEADS
    alog_pad = jnp.zeros((1, LANES), F32).at[0, :HEADS].set(a_log)
    dtb_pad = jnp.zeros((1, LANES), F32).at[0, :HEADS].set(dt_bias)
    row = lambda b, c: b * nc + c
    return pl.pallas_call(
        _gdn_chunk_kernel,
        grid=(batch, nc),
        in_specs=[pl.BlockSpec((GDN_CHUNK, D_HEADS), lambda b, c: (row(b, c), 0)),
                  pl.BlockSpec((GDN_CHUNK, D_HEADS), lambda b, c: (row(b, c), 1)),
                  pl.BlockSpec((GDN_CHUNK, D_HEADS), lambda b, c: (row(b, c), 2)),
                  pl.BlockSpec((GDN_CHUNK, D_HEADS), lambda b, c: (row(b, c), zcol)),
                  pl.BlockSpec((GDN_CHUNK, LANES), lambda b, c: (row(b, c), N_SB // LANES)),
                  pl.BlockSpec((1, LANES), lambda b, c: (0, 0)),
                  pl.BlockSpec((1, LANES), lambda b, c: (0, 0)),
                  pl.BlockSpec((1, HEAD_DIM), lambda b, c: (0, 0))],
        out_specs=pl.BlockSpec((GDN_CHUNK, D_HEADS), lambda b, c: (row(b, c), 0)),
        out_shape=jax.ShapeDtypeStruct((tp, D_HEADS), BF16),
        scratch_shapes=[pltpu.VMEM((HEADS, HEAD_DIM, HEAD_DIM), F32)],
        compiler_params=_cparams(("parallel", "arbitrary"), 48),
        name="gdn_chunks",
    )(qkv, qkv, qkv, proj_main, proj_side, alog_pad, dtb_pad, norm_g.reshape(1, HEAD_DIM))


SB_HEADS_PER_STEP = 8


def _sb_kernel(q_ref, k_ref, v_ref, o_ref, kb_ref, vb_ref, acc_ref):
    i = pl.program_id(2)
    T = ROW_BLOCK
    lp = k_ref.shape[0]
    heads = range(SB_HEADS_PER_STEP)
    sls = [slice(h * HEAD_DIM, (h + 1) * HEAD_DIM) for h in heads]

    @pl.when(i == 0)
    def _():
        real = lax.broadcasted_iota(I32, (lp, 1), 0) >= PAD_ROWS
        kb_ref[...] = k_ref[...].astype(BF16)
        vb_ref[...] = jnp.where(real, v_ref[...], 0.0).astype(BF16)

    q = [(q_ref[:, sls[h]] * (HEAD_DIM ** -0.5)).astype(BF16) for h in heads]
    rowi = lax.broadcasted_iota(I32, (T, T), 0)
    coli = lax.broadcasted_iota(I32, (T, T), 1)
    earlier = coli < rowi
    suffix = (rowi > coli).astype(BF16)

    def key_block(j, later, diagonal):
        start = pl.multiple_of(j * T, T)
        z = [_bdot_nt(q[h], kb_ref[pl.ds(start, T), sls[h]]) for h in heads]
        sp = [_softplus(z[h]) for h in heads]
        lom = [jnp.where(earlier, -sp[h], 0.0) if diagonal else -sp[h] for h in heads]
        parts = [_split_bf16(lom[h], 2) for h in heads]
        between = [later[h] + jnp.dot(parts[h][0], suffix, preferred_element_type=F32)
                   + jnp.dot(parts[h][1], suffix, preferred_element_type=F32) for h in heads]
        attn = [jnp.exp(z[h] - sp[h] + between[h]) for h in heads]
        if diagonal:
            attn = [jnp.where(earlier, attn[h], 0.0) for h in heads]
        pv = [_bdot(attn[h], vb_ref[pl.ds(start, T), sls[h]]) for h in heads]
        for h in heads:
            if diagonal:
                acc_ref[h] = pv[h]
            else:
                acc_ref[h] += pv[h]
        return tuple(later[h] + jnp.sum(lom[h], -1, keepdims=True) for h in heads)

    later = key_block(i, tuple(jnp.zeros((T, 1), F32) for _ in heads), True)
    lax.fori_loop(1, i + 1, lambda n, later: key_block(i - n, later, False), later)
    for h in heads:
        o_ref[:, sls[h]] = acc_ref[h].astype(o_ref.dtype)


def _stick_breaking(proj_side, batch, lp):
    tp = proj_side.shape[0]
    nq = lp // ROW_BLOCK
    hb = SB_HEADS_PER_STEP
    w = hb * HEAD_DIM
    ng = HEADS // hb
    return pl.pallas_call(
        _sb_kernel,
        grid=(batch, ng, nq),
        in_specs=[pl.BlockSpec((ROW_BLOCK, w), lambda b, g, i: (b * nq + i, g)),
                  pl.BlockSpec((lp, w), lambda b, g, i: (b, ng + g)),
                  pl.BlockSpec((lp, w), lambda b, g, i: (b, 2 * ng + g))],
        out_specs=pl.BlockSpec((ROW_BLOCK, w), lambda b, g, i: (b * nq + i, g)),
        out_shape=jax.ShapeDtypeStruct((tp, D_HEADS), BF16),
        scratch_shapes=[pltpu.VMEM((lp, w), BF16), pltpu.VMEM((lp, w), BF16),
                        pltpu.VMEM((hb, ROW_BLOCK, HEAD_DIM), F32)],
        compiler_params=_cparams(("parallel", "parallel", "arbitrary"), 58),
        name="stick_breaking",
    )(proj_side, proj_side, proj_side)


def _merge_kernel(h_ref, ya_ref, yb_ref, yc_ref, wg0_ref, wg1_ref, wg2_ref, bg0_ref, bg1_ref, bg2_ref,
                  wb_ref, o_ref, wg_s, wb_s):
    @pl.when(pl.program_id(1) == 0)
    def _():
        wg_s[0] = wg0_ref[...].astype(BF16)
        wg_s[1] = wg1_ref[...].astype(BF16)
        wg_s[2] = wg2_ref[...].astype(BF16)
        wb_s[...] = wb_ref[...].astype(BF16)

    h = h_ref[...]
    acc = None
    for n, (y_ref, bg_ref) in enumerate(((ya_ref, bg0_ref), (yb_ref, bg1_ref), (yc_ref, bg2_ref))):
        gate = jax.nn.sigmoid(jnp.dot(h, wg_s[n], preferred_element_type=F32) + bg_ref[...])
        up = jnp.dot(y_ref[...], wb_s[n], preferred_element_type=F32)
        acc = gate * up if acc is None else acc + gate * up
    o_ref[...] = acc.astype(o_ref.dtype)


def _merge(h16, ya, yb, yc, w_gate, b_gate, w_branch, l, tn=256, tm=512):
    tp = h16.shape[0]
    nj = D_MODEL // tn
    wg_spec = lambda n: pl.BlockSpec((None, D_MODEL, tn), lambda j, i: (l, 0, n * nj + j))
    bg_spec = lambda n: pl.BlockSpec((None, 1, tn), lambda j, i: (l, 0, n * nj + j))
    y_spec = pl.BlockSpec((tm, D_HEADS), lambda j, i: (i, 0))
    return pl.pallas_call(
        _merge_kernel,
        grid=(nj, tp // tm),
        in_specs=[pl.BlockSpec((tm, D_MODEL), lambda j, i: (i, 0)), y_spec, y_spec, y_spec,
                  wg_spec(0), wg_spec(1), wg_spec(2), bg_spec(0), bg_spec(1), bg_spec(2),
                  pl.BlockSpec((None, N_BRANCH, D_HEADS, tn), lambda j, i: (l, 0, 0, j))],
        out_specs=pl.BlockSpec((tm, tn), lambda j, i: (i, j)),
        out_shape=jax.ShapeDtypeStruct((tp, D_MODEL), BF16),
        scratch_shapes=[pltpu.VMEM((N_BRANCH, D_MODEL, tn), BF16),
                        pltpu.VMEM((N_BRANCH, D_HEADS, tn), BF16)],
        compiler_params=_cparams(("arbitrary", "arbitrary"), 56),
        name="branch_merge",
    )(h16, ya, yb, yc, w_gate, w_gate, w_gate, b_gate, b_gate, b_gate, w_branch)


ROW_TILES = D_MODEL // (2 * LANES)
HIGH_HALF = -65536


def _store_rows_packed(ref, x):
    rows = x.shape[0]
    half = D_MODEL // 2
    bits = lax.bitcast_convert_type(x.astype(BF16).astype(F32), I32)
    words = lax.shift_right_logical(bits[:, :half], 16) | bits[:, half:]
    for c in range(ROW_TILES):
        ref[pl.ds(c, rows, stride=ROW_TILES), :] = words[:, c * LANES:(c + 1) * LANES]


def _load_rows_packed(ref, first_row, rows):
    words = [ref[pl.ds(first_row * ROW_TILES + c, rows, stride=ROW_TILES), :] for c in range(ROW_TILES)]
    low = [lax.bitcast_convert_type(lax.shift_left(w, 16), F32) for w in words]
    high = [lax.bitcast_convert_type(w & HIGH_HALF, F32) for w in words]
    return jnp.concatenate(low + high, axis=-1)


def _oproj_ln_kernel(m_ref, w_ref, h_ref, g_ref, b_ref, ho_ref, h16o_ref, hpk_ref):
    mix = jnp.dot(m_ref[...], w_ref[...], preferred_element_type=F32)
    y = _ln_rows(DEEPNORM_ALPHA * h_ref[...] + mix, g_ref[...], b_ref[...])
    ho_ref[...] = y
    h16o_ref[...] = y.astype(BF16)
    _store_rows_packed(hpk_ref, y)


def _oproj_ln(merged, w_o16, h, g, b, tm=256):
    tp, d = h.shape
    assert tp % tm == 0
    row = pl.BlockSpec((tm, d), lambda i: (i, 0))
    vec = pl.BlockSpec((1, d), lambda i: (0, 0))
    return pl.pallas_call(
        _oproj_ln_kernel,
        grid=(tp // tm,),
        in_specs=[row, pl.BlockSpec((d, d), lambda i: (0, 0)), row, vec, vec],
        out_specs=[row, row, pl.BlockSpec((tm * ROW_TILES, LANES), lambda i: (i, 0))],
        out_shape=[jax.ShapeDtypeStruct((tp, d), F32), jax.ShapeDtypeStruct((tp, d), BF16),
                   jax.ShapeDtypeStruct((tp * ROW_TILES, LANES), I32)],
        compiler_params=_cparams(("parallel",), 48),
        name="oproj_ln",
    )(merged, w_o16, h, g.reshape(1, d), b.reshape(1, d))


def _router_kernel(h_ref, w_ref, b_ref, idx_ref, wts_ref, rank_ref, cnt_ref, base_s):
    i = pl.program_id(0)
    tm = h_ref.shape[0]

    @pl.when(i == 0)
    def _():
        base_s[...] = jnp.zeros_like(base_s)

    xh, xl = _split_bf16(h_ref[...], 2)
    wh, wl = _split_bf16(w_ref[...], 2)
    dot = functools.partial(jnp.dot, preferred_element_type=F32)
    scores = jax.nn.sigmoid(dot(xh, wh) + dot(xh, wl) + dot(xl, wh))
    lane = lax.broadcasted_iota(I32, (tm, LANES), 1).astype(F32)
    sel = jnp.where(lane < N_EXPERTS, scores + b_ref[...], NEG_INF)
    chosen = jnp.zeros((tm, LANES), F32)
    idx_out = jnp.zeros((tm, LANES), F32)
    sc_out = jnp.zeros((tm, LANES), F32)
    picks = []
    for k in range(TOP_K):
        m = jnp.max(sel, -1, keepdims=True)
        ik = jnp.min(jnp.where(sel == m, lane, float(LANES)), -1, keepdims=True)
        hit = lane == ik
        sk = jnp.sum(jnp.where(hit, scores, 0.0), -1, keepdims=True)
        sel = jnp.where(hit, NEG_INF, sel)
        chosen = jnp.where(hit, 1.0, chosen)
        idx_out = jnp.where(lane == k, ik, idx_out)
        sc_out = jnp.where(lane == k, sk, sc_out)
        picks.append(hit)
    total = jnp.sum(sc_out, -1, keepdims=True)
    wts_ref[...] = sc_out / total * ROUTED_SCALE
    idx_ref[...] = idx_out.T[0:TOP_K, :].astype(I32)
    before = (lax.broadcasted_iota(I32, (tm, tm), 0) > lax.broadcasted_iota(I32, (tm, tm), 1)).astype(BF16)
    count_before = dot(before, chosen.astype(BF16)) + base_s[0:1, :]
    rank_out = jnp.zeros((tm, LANES), F32)
    for k in range(TOP_K):
        rk = jnp.sum(jnp.where(picks[k], count_before, 0.0), -1, keepdims=True)
        rank_out = jnp.where(lane == k, rk, rank_out)
    rank_ref[...] = rank_out.T[0:TOP_K, :].astype(I32)
    base_s[...] = base_s[...] + jnp.sum(chosen, 0, keepdims=True)
    cnt_ref[...] = base_s[...].astype(I32)


def _router(h, router_w, router_bias, tm=ROW_BLOCK):
    tp, d = h.shape
    w_pad = jnp.zeros((d, LANES), F32).at[:, :N_EXPERTS].set(router_w)
    b_pad = jnp.zeros((1, LANES), F32).at[0, :N_EXPERTS].set(router_bias)
    tok = pl.BlockSpec((tm, LANES), lambda i: (i, 0))
    per_k = pl.BlockSpec((TOP_K, tm), lambda i: (0, i))
    return pl.pallas_call(
        _router_kernel,
        grid=(tp // tm,),
        in_specs=[pl.BlockSpec((tm, d), lambda i: (i, 0)),
                  pl.BlockSpec((d, LANES), lambda i: (0, 0)),
                  pl.BlockSpec((1, LANES), lambda i: (0, 0))],
        out_specs=[per_k, tok, per_k, pl.BlockSpec((8, LANES), lambda i: (0, 0))],
        out_shape=[jax.ShapeDtypeStruct((TOP_K, tp), I32), jax.ShapeDtypeStruct((tp, LANES), F32),
                   jax.ShapeDtypeStruct((TOP_K, tp), I32), jax.ShapeDtypeStruct((8, LANES), I32)],
        scratch_shapes=[pltpu.VMEM((8, LANES), F32)],
        compiler_params=_cparams(("arbitrary",)),
        name="moe_router",
    )(h, w_pad, b_pad)


MOE_ROWS = 256
DMA_UNROLL = 8


def _row_span(r):
    return pl.ds(pl.multiple_of(r * ROW_TILES, ROW_TILES), ROW_TILES)


def _ffn_kernel(be_ref, nu_ref, tok0_ref, tokn_ref, dstp_ref, dstl_ref, h_hbm, wg_ref, wu_ref, wd_ref, y_hbm,
                x0, x1, y0, y1, gsem, ssem, wg_s, wu_s, wd_s):
    i = pl.program_id(0)
    n_used = nu_ref[0]
    even = i % 2 == 0
    whole = pl.ds(0, MOE_ROWS * ROW_TILES)

    def gather_row(tok_ref, r, x_next, sem):
        pltpu.make_async_copy(h_hbm.at[_row_span(tok_ref[0, 0, r]), :], x_next.at[_row_span(r), :], sem).start()

    def scatter_row(dst_ref, r, y_from, sem):
        pltpu.make_async_copy(y_from.at[_row_span(r), :], y_hbm.at[_row_span(dst_ref[0, 0, r]), :], sem).start()

    def rolled(fn):
        def body(g, carry):
            for u in range(DMA_UNROLL):
                fn(g * DMA_UNROLL + u)
            return carry

        lax.fori_loop(0, MOE_ROWS // DMA_UNROLL, body, 0)

    def gather_wait(x_buf, sem):
        pltpu.make_async_copy(h_hbm.at[whole, :], x_buf, sem).wait()

    def scatter_wait(y_buf, sem):
        pltpu.make_async_copy(y_buf, y_hbm.at[whole, :], sem).wait()

    @pl.when(i == 0)
    def _():
        y1[...] = jnp.zeros_like(y1)
        rolled(lambda r: gather_row(tok0_ref, r, x0, gsem.at[0]))

    prev = be_ref[jnp.maximum(i - 1, 0)]

    @pl.when((i == 0) | (be_ref[i] != prev))
    def _():
        wg_s[...] = wg_ref[...].astype(BF16)
        wu_s[...] = wu_ref[...].astype(BF16)
        wd_s[...] = wd_ref[...].astype(BF16)

    def step(x_cur, x_next, y_cur, y_prev, cur, nxt):
        gather_wait(x_cur, gsem.at[cur])

        @pl.when(i >= 1)
        def _():
            scatter_wait(y_cur, ssem.at[cur])

        for r in range(MOE_ROWS):
            scatter_row(dstp_ref, r, y_prev, ssem.at[nxt])
        for r in range(MOE_ROWS):
            gather_row(tokn_ref, r, x_next, gsem.at[nxt])
        x = _load_rows_packed(x_cur, 0, MOE_ROWS).astype(BF16)
        hid = _silu(jnp.dot(x, wg_s[...], preferred_element_type=F32)) * jnp.dot(
            x, wu_s[...], preferred_element_type=F32)
        _store_rows_packed(y_cur, jnp.dot(hid.astype(BF16), wd_s[...], preferred_element_type=F32))

        @pl.when(i == n_used - 1)
        def _():
            gather_wait(x_next, gsem.at[nxt])
            scatter_wait(y_prev, ssem.at[nxt])
            rolled(lambda r: scatter_row(dstl_ref, r, y_cur, ssem.at[cur]))
            scatter_wait(y_cur, ssem.at[cur])

    @pl.when(even & (i < n_used))
    def _():
        step(x0, x1, y0, y1, 0, 1)

    @pl.when(jnp.logical_not(even) & (i < n_used))
    def _():
        step(x1, x0, y1, y0, 1, 0)


def _expert_ffn(h_pk, row_tok, row_dst, block_e, n_used, e_gate, e_up, e_down, l):
    d = D_MODEL
    tokens = h_pk.shape[0] // ROW_TILES
    nb = block_e.shape[0]
    tok3 = row_tok.reshape(nb, 1, MOE_ROWS)
    spill = TOP_K * tokens + jnp.arange(MOE_ROWS, dtype=I32)
    dst3 = jnp.concatenate([spill, row_dst]).reshape(nb + 1, 1, MOE_ROWS)
    smem_blk = lambda f: pl.BlockSpec((1, 1, MOE_ROWS), f, memory_space=pltpu.SMEM)
    buf = pltpu.VMEM((MOE_ROWS * ROW_TILES, LANES), I32)
    grid_spec = pltpu.PrefetchScalarGridSpec(
        num_scalar_prefetch=2,
        grid=(nb,),
        in_specs=[smem_blk(lambda i, be, nu: (0, 0, 0)),
                  smem_blk(lambda i, be, nu: (jnp.minimum(i + 1, nb - 1), 0, 0)),
                  smem_blk(lambda i, be, nu: (i, 0, 0)),
                  smem_blk(lambda i, be, nu: (nu[0], 0, 0)),
                  pl.BlockSpec(memory_space=pl.ANY),
                  pl.BlockSpec((None, None, d, D_EXPERT), lambda i, be, nu: (l, be[i], 0, 0)),
                  pl.BlockSpec((None, None, d, D_EXPERT), lambda i, be, nu: (l, be[i], 0, 0)),
                  pl.BlockSpec((None, None, D_EXPERT, d), lambda i, be, nu: (l, be[i], 0, 0))],
        out_specs=pl.BlockSpec(memory_space=pl.ANY),
        scratch_shapes=[buf, buf, buf, buf,
                        pltpu.SemaphoreType.DMA((2,)),
                        pltpu.SemaphoreType.DMA((2,)),
                        pltpu.VMEM((d, D_EXPERT), BF16),
                        pltpu.VMEM((d, D_EXPERT), BF16),
                        pltpu.VMEM((D_EXPERT, d), BF16)],
    )
    return pl.pallas_call(
        _ffn_kernel,
        grid_spec=grid_spec,
        out_shape=jax.ShapeDtypeStruct(((TOP_K * tokens + MOE_ROWS) * ROW_TILES, LANES), I32),
        compiler_params=_cparams(("arbitrary",), 52),
        name="expert_ffn",
    )(block_e, n_used, tok3, tok3, dst3, dst3, h_pk, e_gate, e_up, e_down)


def _combine_kernel(*refs, final):
    y_refs = refs[:TOP_K]
    h_ref, h16_ref, wts_ref, sg_ref, su_ref, sd_ref, g_ref, b_ref = refs[TOP_K:TOP_K + 8]
    out_refs = refs[TOP_K + 8:]
    tm = h_ref.shape[0]
    x = h16_ref[...]
    hid = _silu(jnp.dot(x, sg_ref[...], preferred_element_type=F32)) * jnp.dot(
        x, su_ref[...], preferred_element_type=F32)
    acc = jnp.dot(hid.astype(BF16), sd_ref[...], preferred_element_type=F32)
    wts = wts_ref[...]
    for k in range(TOP_K):
        acc = acc + _load_rows_packed(y_refs[k], 0, tm) * wts[:, k:k + 1]
    y = _ln_rows(DEEPNORM_ALPHA * h_ref[...] + acc, g_ref[...], b_ref[...])
    out_refs[0][...] = y
    if not final:
        out_refs[1][...] = y.astype(BF16)


def _combine(y_lin, h, h16, wts, sg16, su16, sd16, g, b, lp, final):
    tp, d = h.shape
    tm = ROW_BLOCK
    nt = tp // tm
    bps = lp // tm
    row = pl.BlockSpec((tm, d), lambda i: (i, 0))
    vec = pl.BlockSpec((1, d), lambda i: (0, 0))
    y_specs = [pl.BlockSpec((tm * ROW_TILES, LANES), lambda i, k=k: (k * nt + i, 0)) for k in range(TOP_K)]
    if final:
        out_specs = [pl.BlockSpec((tm, d), lambda i: (_frame_block(i, bps), 0))]
        out_shape = [jax.ShapeDtypeStruct((tp // lp * (lp - tm), d), F32)]
    else:
        out_specs = [row, row]
        out_shape = [jax.ShapeDtypeStruct((tp, d), F32), jax.ShapeDtypeStruct((tp, d), BF16)]
    return pl.pallas_call(
        functools.partial(_combine_kernel, final=final),
        grid=(nt,),
        in_specs=y_specs + [row, row,
                            pl.BlockSpec((tm, LANES), lambda i: (i, 0)),
                            pl.BlockSpec((d, D_EXPERT), lambda i: (0, 0)),
                            pl.BlockSpec((d, D_EXPERT), lambda i: (0, 0)),
                            pl.BlockSpec((D_EXPERT, d), lambda i: (0, 0)),
                            vec, vec],
        out_specs=out_specs,
        out_shape=out_shape,
        compiler_params=_cparams(("arbitrary",), 56),
        name="moe_combine",
    )(*([y_lin] * TOP_K), h, h16, wts, sg16, su16, sd16, g.reshape(1, d), b.reshape(1, d))


def _routing_tables(idx, rank, counts):
    tp = idx.shape[1]
    n_blocks = -(-(tp * TOP_K) // MOE_ROWS) + N_EXPERTS
    n_rows = n_blocks * MOE_ROWS
    padded = (counts + MOE_ROWS - 1) // MOE_ROWS * MOE_ROWS
    pad_end = jnp.cumsum(padded)
    pad_start = pad_end - padded
    dest = pad_start[idx] + rank
    info = jnp.arange(tp, dtype=I32)[None, :] * TOP_K + jnp.arange(TOP_K, dtype=I32)[:, None] + 1
    row_info = jnp.zeros((n_rows,), I32).at[dest.reshape(-1)].set(info.reshape(-1), unique_indices=True)
    filled = row_info > 0
    tok = (row_info - 1) >> 3
    k = (row_info - 1) & (TOP_K - 1)
    row_tok = jnp.where(filled, tok, 0)
    row_dst = jnp.where(filled, k * tp + tok, TOP_K * tp + jnp.arange(n_rows, dtype=I32) % MOE_ROWS)
    block_start = jnp.arange(n_blocks, dtype=I32) * MOE_ROWS
    block_e = jnp.minimum(jnp.sum((pad_end[None, :] <= block_start[:, None]).astype(I32), axis=1), N_EXPERTS - 1)
    n_used = (pad_end[-1] // MOE_ROWS).astype(I32).reshape(1)
    return row_tok, row_dst, block_e, n_used


def kernel(x, meta, ln_in_g, ln_in_b, w_in, pool_w, pool_scale, conv_w, A_log, dt_bias, gdn_norm_g, w_branch, w_gate, b_gate, w_o, ln1_g, ln1_b, router_w, router_bias, exp_w_gate, exp_w_up, exp_w_down, sh_w_gate, sh_w_up, sh_w_down, ln2_g, ln2_b):
    batch, seq, d = x.shape
    depth = w_in.shape[0]
    lp = PAD_ROWS + N_META + seq
    assert seq % ROW_BLOCK == 0 and d == D_MODEL
    tp = batch * lp
    h, h16 = _ln_in(x.reshape(batch * seq, d), meta, ln_in_g, ln_in_b, lp)
    b_gate3 = b_gate.reshape(depth, 1, N_BRANCH * d)
    tm = next(t for t in (1088, 512, ROW_BLOCK) if tp % t == 0)

    w_main = w_in[:, :, :N_MAIN]
    w_side = jnp.concatenate([w_in[:, :, N_MAIN + N_GATES:], w_in[:, :, N_MAIN:N_MAIN + N_GATES],
                              jnp.zeros((depth, d, LANES - N_GATES), F32)], axis=2)

    for l in range(depth):
        proj_main = _matmul(h16, w_main, pl.BlockSpec((None, d, 512), lambda j, i: (l, 0, j)),
                            N_MAIN, 512, tm, name="proj_main")
        proj_side = _matmul(h16, w_side, pl.BlockSpec((None, d, 640), lambda j, i: (l, 0, j)),
                            N_SIDE, 640, tm, name="proj_side")
        y_a = _pool_mixer(proj_main, pool_w[l], pool_scale[l], batch, lp)
        qkv = _gdn_conv(proj_main, conv_w[l], batch, lp)
        y_b = _gdn(qkv, proj_main, proj_side, A_log[l], dt_bias[l], gdn_norm_g[l], batch, lp)
        y_c = _stick_breaking(proj_side, batch, lp)
        merged = _merge(h16, y_a, y_b, y_c, w_gate, b_gate3, w_branch, l, tm=tm)
        h, h16, h_pk = _oproj_ln(merged, w_o[l].astype(BF16), h, ln1_g[l], ln1_b[l],
                                 tm=256 if tp % 256 == 0 else ROW_BLOCK)

        idx, wts, rank, cnt = _router(h, router_w[l], router_bias[l])
        row_tok, row_dst, block_e, n_used = _routing_tables(idx, rank, cnt[0, :N_EXPERTS])
        y_pk = _expert_ffn(h_pk, row_tok, row_dst, block_e, n_used, exp_w_gate, exp_w_up, exp_w_down, l)
        out = _combine(y_pk, h, h16, wts, sh_w_gate[l].astype(BF16), sh_w_up[l].astype(BF16),
                       sh_w_down[l].astype(BF16), ln2_g[l], ln2_b[l], lp, final=(l == depth - 1))
        if l < depth - 1:
            h, h16 = out

    return out[0].reshape(batch, seq, d)
```

```python
import functools
import math

import jax
import jax.numpy as jnp
from jax import lax
from jax.experimental import pallas as pl
from jax.experimental.pallas import tpu as pltpu

F32 = jnp.float32
BF16 = jnp.bfloat16
I32 = jnp.int32

D_MODEL = 2048
DEPTH = 4
N_META = 16
ROW_BLOCK = 128
PAD_ROWS = ROW_BLOCK - N_META
POOL_GROUP = 256
N_POOL_GROUPS = 4
D_POOL = POOL_GROUP * N_POOL_GROUPS
HEADS = 8
HEAD_DIM = 128
D_HEADS = HEADS * HEAD_DIM
GDN_CONV = 4
GDN_CHUNK = 64
N_BRANCH = 3
N_EXPERTS = 64
TOP_K = 8
D_EXPERT = 384
ROUTED_SCALE = 2.5
DEEPNORM_ALPHA = (2 * DEPTH) ** 0.25
LN_EPS = 1e-5
RMS_EPS = 1e-6
LANES = 128
N_MAIN = D_POOL + 4 * D_HEADS
N_GATES = 2 * HEADS
N_SB = 3 * D_HEADS
N_SIDE = N_SB + LANES
NEG_INF = float("-inf")


def _cparams(semantics, vmem_mib=None):
    kw = {}
    if vmem_mib is not None:
        kw["vmem_limit_bytes"] = vmem_mib * 2**20
    return pltpu.CompilerParams(dimension_semantics=semantics, **kw)


def _bdot(a, b):
    return jnp.dot(a.astype(BF16), b.astype(BF16), preferred_element_type=F32)


def _bdot_nt(a, b):
    return lax.dot_general(a.astype(BF16), b.astype(BF16), (((1,), (1,)), ((), ())),
                           preferred_element_type=F32)


def _split_bf16(x, parts):
    out, r = [], x
    for _ in range(parts):
        p = r.astype(BF16)
        out.append(p)
        r = r - p.astype(F32)
    return out


def _ln_rows(x, g, b):
    mu = jnp.mean(x, -1, keepdims=True)
    xc = x - mu
    var = jnp.mean(xc * xc, -1, keepdims=True)
    return xc * lax.rsqrt(var + LN_EPS) * g + b


def _silu(x):
    return x * jax.nn.sigmoid(x)


def _softplus(x):
    return jnp.maximum(x, 0.0) + jnp.log1p(jnp.exp(-jnp.abs(x)))


def _ln_in_kernel(x_ref, meta_ref, g_ref, b_ref, h_ref, h16_ref, *, blocks_per_seq):
    first = pl.program_id(0) % blocks_per_seq == 0
    d = x_ref.shape[1]
    lead = jnp.concatenate([jnp.zeros((PAD_ROWS, d), F32), meta_ref[...]], axis=0)
    y = _ln_rows(jnp.where(first, lead, x_ref[...]), g_ref[...], b_ref[...])
    h_ref[...] = y
    h16_ref[...] = y.astype(BF16)


def _frame_block(i, blocks_per_seq):
    b = i // blocks_per_seq
    j = i - b * blocks_per_seq
    return b * (blocks_per_seq - 1) + jnp.maximum(j - 1, 0)


def _ln_in(x2, meta, g, b, lp):
    n, d = x2.shape
    tm = ROW_BLOCK
    bps = lp // tm
    tp = n // (bps - 1) // tm * lp
    row = pl.BlockSpec((tm, d), lambda i: (i, 0))
    vec = pl.BlockSpec((1, d), lambda i: (0, 0))
    return pl.pallas_call(
        functools.partial(_ln_in_kernel, blocks_per_seq=bps),
        grid=(tp // tm,),
        in_specs=[pl.BlockSpec((tm, d), lambda i: (_frame_block(i, bps), 0)),
                  pl.BlockSpec((N_META, d), lambda i: (0, 0)), vec, vec],
        out_specs=[row, row],
        out_shape=[jax.ShapeDtypeStruct((tp, d), F32), jax.ShapeDtypeStruct((tp, d), BF16)],
        compiler_params=_cparams(("arbitrary",)),
        name="ln_in",
    )(x2, meta, g.reshape(1, d), b.reshape(1, d))


def _mm_kernel(x_ref, w_ref, o_ref, wb_ref):
    @pl.when(pl.program_id(1) == 0)
    def _():
        wb_ref[...] = w_ref[...].astype(BF16)

    o_ref[...] = jnp.dot(x_ref[...], wb_ref[...], preferred_element_type=F32).astype(o_ref.dtype)


def _matmul(x16, w, w_spec, n_out, tn, tm=512, name="mm"):
    m, k = x16.shape
    return pl.pallas_call(
        _mm_kernel,
        grid=(n_out // tn, m // tm),
        in_specs=[pl.BlockSpec((tm, k), lambda j, i: (i, 0)), w_spec],
        out_specs=pl.BlockSpec((tm, tn), lambda j, i: (i, j)),
        out_shape=jax.ShapeDtypeStruct((m, n_out), F32),
        scratch_shapes=[pltpu.VMEM((k, tn), BF16)],
        compiler_params=_cparams(("arbitrary", "arbitrary"), 48),
        name=name,
    )(x16, w)


def _pool_kernel(u_ref, w_ref, s_ref, o_ref):
    g = pl.program_id(1)
    lp = u_ref.shape[0]
    t = lax.broadcasted_iota(I32, (lp, 1), 0) - PAD_ROWS
    u = jnp.where(t >= 0, u_ref[...], 0.0)
    s2 = u + pltpu.roll(u, 1, 0)
    s4 = s2 + pltpu.roll(s2, 2, 0)
    s8 = s4 + pltpu.roll(s4, 4, 0)
    s16 = s8 + pltpu.roll(s8, 8, 0)
    s = jnp.where(g == 0, s2, jnp.where(g == 1, s4, jnp.where(g == 2, s8, s16)))
    window = jnp.left_shift(2, g)
    cnt = jnp.maximum(jnp.minimum(t + 1, window), 1).astype(F32)
    pooled = s / cnt - u
    y = _bdot(pooled, w_ref[...]) * s_ref[...]
    o_ref[...] = y.astype(o_ref.dtype)


def _pool_mixer(proj_main, pool_w, pool_scale, batch, lp):
    tp = proj_main.shape[0]
    return pl.pallas_call(
        _pool_kernel,
        grid=(batch, N_POOL_GROUPS),
        in_specs=[pl.BlockSpec((lp, POOL_GROUP), lambda b, g: (b, g)),
                  pl.BlockSpec((None, POOL_GROUP, POOL_GROUP), lambda b, g: (g, 0, 0)),
                  pl.BlockSpec((1, POOL_GROUP), lambda b, g: (0, g))],
        out_specs=pl.BlockSpec((lp, POOL_GROUP), lambda b, g: (b, g)),
        out_shape=jax.ShapeDtypeStruct((tp, D_POOL), BF16),
        compiler_params=_cparams(("parallel", "parallel"), 48),
        name="pool_mixer",
    )(proj_main, pool_w, pool_scale.reshape(1, D_POOL))


def _gdn_conv_kernel(u_ref, w_ref, o_ref):
    c = pl.program_id(1)
    lp = u_ref.shape[0]
    t = lax.broadcasted_iota(I32, (lp, 1), 0) - PAD_ROWS
    u = jnp.where(t >= 0, u_ref[...], 0.0)
    w = w_ref[...]
    y = (u * w[3:4] + pltpu.roll(u, 1, 0) * w[2:3] + pltpu.roll(u, 2, 0) * w[1:2]
         + pltpu.roll(u, 3, 0) * w[0:1])
    y = _silu(y)
    nrm = y * lax.rsqrt(jnp.sum(y * y, -1, keepdims=True) + RMS_EPS)
    o_ref[...] = jnp.where(c < HEADS, nrm * (HEAD_DIM ** -0.5), jnp.where(c < 2 * HEADS, nrm, y))


def _gdn_conv(proj_main, conv_w, batch, lp):
    tp = proj_main.shape[0]
    first = D_POOL // HEAD_DIM
    return pl.pallas_call(
        _gdn_conv_kernel,
        grid=(batch, 3 * HEADS),
        in_specs=[pl.BlockSpec((lp, HEAD_DIM), lambda b, c: (b, first + c)),
                  pl.BlockSpec((GDN_CONV, HEAD_DIM), lambda b, c: (0, c))],
        out_specs=pl.BlockSpec((lp, HEAD_DIM), lambda b, c: (b, c)),
        out_shape=jax.ShapeDtypeStruct((tp, 3 * D_HEADS), F32),
        compiler_params=_cparams(("parallel", "parallel"), 48),
        name="gdn_conv",
    )(proj_main, conv_w)


def _gdn_chunk_kernel(q_ref, k_ref, v_ref, z_ref, ab_ref, alog_ref, dtb_ref, ng_ref, o_ref, s_ref):
    c = pl.program_id(1)
    C = GDN_CHUNK

    @pl.when(c == 0)
    def _():
        s_ref[...] = jnp.zeros_like(s_ref)

    ab = ab_ref[...]
    pos = c * C + lax.broadcasted_iota(I32, (C, 1), 0)
    real = pos >= PAD_ROWS
    g_all = jnp.where(real, -jnp.exp(alog_ref[...]) * _softplus(ab + dtb_ref[...]), 0.0)
    beta_all = jnp.where(real, jax.nn.sigmoid(ab), 0.0)
    ri = lax.broadcasted_iota(I32, (C, C), 0)
    ci = lax.broadcasted_iota(I32, (C, C), 1)
    causal = ri >= ci
    strict = ri > ci
    tri = causal.astype(BF16)
    eye = (ri == ci).astype(F32)
    gcum = sum(jnp.dot(tri, p, preferred_element_type=F32) for p in _split_bf16(g_all, 3))
    gcum_t = jnp.concatenate([gcum, jnp.zeros_like(gcum)], axis=0).T
    ng = ng_ref[...]
    heads = range(HEADS)
    sls = [slice(h * HEAD_DIM, (h + 1) * HEAD_DIM) for h in heads]

    gc_col = [gcum[:, h:h + 1] for h in heads]
    g_last = [gcum[C - 1:C, h:h + 1] for h in heads]
    beta = [beta_all[:, HEADS + h:HEADS + h + 1] for h in heads]
    decay = [jnp.where(causal, jnp.exp(jnp.where(causal, gc_col[h] - gcum_t[h:h + 1, 0:C], 0.0)), 0.0)
             for h in heads]
    eg = [jnp.exp(gc_col[h]) for h in heads]
    qb = [q_ref[:, sls[h]].astype(BF16) for h in heads]
    kf = [k_ref[:, sls[h]] for h in heads]
    kb = [kf[h].astype(BF16) for h in heads]
    kbeta = [kf[h] * beta[h] for h in heads]
    lmat = [jnp.where(strict, _bdot_nt(kbeta[h], kb[h]) * decay[h], 0.0) for h in heads]
    attn = [_bdot_nt(qb[h], kb[h]) * decay[h] for h in heads]
    tinv = [eye - lmat[h] for h in heads]
    pw = [_bdot(lmat[h], lmat[h]) for h in heads]
    for it in range(5):
        tinv = [tinv[h] + _bdot(tinv[h], pw[h]) for h in heads]
        if it < 4:
            pw = [_bdot(pw[h], pw[h]) for h in heads]
    sol = [_bdot(tinv[h], jnp.concatenate([v_ref[:, sls[h]] * beta[h], kbeta[h] * eg[h]], axis=-1))
           for h in heads]
    state = [s_ref[h] for h in heads]
    ws = [_bdot(jnp.concatenate([sol[h][:, HEAD_DIM:], q_ref[:, sls[h]] * eg[h]], axis=0), state[h])
          for h in heads]
    v_new = [sol[h][:, :HEAD_DIM] - ws[h][:C] for h in heads]
    k_dec_t = [(kf[h] * jnp.exp(g_last[h] - gc_col[h])).T for h in heads]
    out = [ws[h][C:] + _bdot(attn[h], v_new[h]) for h in heads]
    for h in heads:
        s_ref[h] = state[h] * jnp.exp(g_last[h]) + _bdot(k_dec_t[h], v_new[h])
    for h in heads:
        o = out[h]
        o = o * lax.rsqrt(jnp.mean(o * o, -1, keepdims=True) + RMS_EPS) * ng
        o_ref[:, sls[h]] = (o * _silu(z_ref[:, sls[h]])).astype(o_ref.dtype)


def _gdn(qkv, proj_main, proj_side, a_log, dt_bias, norm_g, batch, lp):
    tp = qkv.shape[0]
    nc = lp // GDN_CHUNK
    zcol = (D_POOL + 3 * D_HEADS) // D_HEADS
    alog_pad = jnp.zeros((1, LANES), F32).at[0, :HEADS].set(a_log)
    dtb_pad = jnp.zeros((1, LANES), F32).at[0, :HEADS].set(dt_bias)
    row = lambda b, c: b * nc + c
    return pl.pallas_call(
        _gdn_chunk_kernel,
        grid=(batch, nc),
        in_specs=[pl.BlockSpec((GDN_CHUNK, D_HEADS), lambda b, c: (row(b, c), 0)),
                  pl.BlockSpec((GDN_CHUNK, D_HEADS), lambda b, c: (row(b, c), 1)),
                  pl.BlockSpec((GDN_CHUNK, D_HEADS), lambda b, c: (row(b, c), 2)),
                  pl.BlockSpec((GDN_CHUNK, D_HEADS), lambda b, c: (row(b, c), zcol)),
                  pl.BlockSpec((GDN_CHUNK, LANES), lambda b, c: (row(b, c), N_SB // LANES)),
                  pl.BlockSpec((1, LANES), lambda b, c: (0, 0)),
                  pl.BlockSpec((1, LANES), lambda b, c: (0, 0)),
                  pl.BlockSpec((1, HEAD_DIM), lambda b, c: (0, 0))],
        out_specs=pl.BlockSpec((GDN_CHUNK, D_HEADS), lambda b, c: (row(b, c), 0)),
        out_shape=jax.ShapeDtypeStruct((tp, D_HEADS), BF16),
        scratch_shapes=[pltpu.VMEM((HEADS, HEAD_DIM, HEAD_DIM), F32)],
        compiler_params=_cparams(("parallel", "arbitrary"), 48),
        name="gdn_chunks",
    )(qkv, qkv, qkv, proj_main, proj_side, alog_pad, dtb_pad, norm_g.reshape(1, HEAD_DIM))


SB_HEADS_PER_STEP = 8


def _sb_kernel(q_ref, k_ref, v_ref, o_ref, kb_ref, vb_ref, acc_ref):
    i = pl.program_id(2)
    T = ROW_BLOCK
    lp = k_ref.shape[0]
    heads = range(SB_HEADS_PER_STEP)
    sls = [slice(h * HEAD_DIM, (h + 1) * HEAD_DIM) for h in heads]

    @pl.when(i == 0)
    def _():
        real = lax.broadcasted_iota(I32, (lp, 1), 0) >= PAD_ROWS
        kb_ref[...] = k_ref[...].astype(BF16)
        vb_ref[...] = jnp.where(real, v_ref[...], 0.0).astype(BF16)

    q = [(q_ref[:, sls[h]] * (HEAD_DIM ** -0.5)).astype(BF16) for h in heads]
    rowi = lax.broadcasted_iota(I32, (T, T), 0)
    coli = lax.broadcasted_iota(I32, (T, T), 1)
    earlier = coli < rowi
    suffix = (rowi > coli).astype(BF16)

    def key_block(j, later, diagonal):
        start = pl.multiple_of(j * T, T)
        z = [_bdot_nt(q[h], kb_ref[pl.ds(start, T), sls[h]]) for h in heads]
        sp = [_softplus(z[h]) for h in heads]
        lom = [jnp.where(earlier, -sp[h], 0.0) if diagonal else -sp[h] for h in heads]
        parts = [_split_bf16(lom[h], 2) for h in heads]
        between = [later[h] + jnp.dot(parts[h][0], suffix, preferred_element_type=F32)
                   + jnp.dot(parts[h][1], suffix, preferred_element_type=F32) for h in heads]
        attn = [jnp.exp(z[h] - sp[h] + between[h]) for h in heads]
        if diagonal:
            attn = [jnp.where(earlier, attn[h], 0.0) for h in heads]
        pv = [_bdot(attn[h], vb_ref[pl.ds(start, T), sls[h]]) for h in heads]
        for h in heads:
            if diagonal:
                acc_ref[h] = pv[h]
            else:
                acc_ref[h] += pv[h]
        return tuple(later[h] + jnp.sum(lom[h], -1, keepdims=True) for h in heads)

    later = key_block(i, tuple(jnp.zeros((T, 1), F32) for _ in heads), True)
    lax.fori_loop(1, i + 1, lambda n, later: key_block(i - n, later, False), later)
    for h in heads:
        o_ref[:, sls[h]] = acc_ref[h].astype(o_ref.dtype)


def _stick_breaking(proj_side, batch, lp):
    tp = proj_side.shape[0]
    nq = lp // ROW_BLOCK
    hb = SB_HEADS_PER_STEP
    w = hb * HEAD_DIM
    ng = HEADS // hb
    return pl.pallas_call(
        _sb_kernel,
        grid=(batch, ng, nq),
        in_specs=[pl.BlockSpec((ROW_BLOCK, w), lambda b, g, i: (b * nq + i, g)),
                  pl.BlockSpec((lp, w), lambda b, g, i: (b, ng + g)),
                  pl.BlockSpec((lp, w), lambda b, g, i: (b, 2 * ng + g))],
        out_specs=pl.BlockSpec((ROW_BLOCK, w), lambda b, g, i: (b * nq + i, g)),
        out_shape=jax.ShapeDtypeStruct((tp, D_HEADS), BF16),
        scratch_shapes=[pltpu.VMEM((lp, w), BF16), pltpu.VMEM((lp, w), BF16),
                        pltpu.VMEM((hb, ROW_BLOCK, HEAD_DIM), F32)],
        compiler_params=_cparams(("parallel", "parallel", "arbitrary"), 58),
        name="stick_breaking",
    )(proj_side, proj_side, proj_side)


def _merge_kernel(h_ref, ya_ref, yb_ref, yc_ref, wg0_ref, wg1_ref, wg2_ref, bg0_ref, bg1_ref, bg2_ref,
                  wb_ref, o_ref, wg_s, wb_s):
    @pl.when(pl.program_id(1) == 0)
    def _():
        wg_s[0] = wg0_ref[...].astype(BF16)
        wg_s[1] = wg1_ref[...].astype(BF16)
        wg_s[2] = wg2_ref[...].astype(BF16)
        wb_s[...] = wb_ref[...].astype(BF16)

    h = h_ref[...]
    acc = None
    for n, (y_ref, bg_ref) in enumerate(((ya_ref, bg0_ref), (yb_ref, bg1_ref), (yc_ref, bg2_ref))):
        gate = jax.nn.sigmoid(jnp.dot(h, wg_s[n], preferred_element_type=F32) + bg_ref[...])
        up = jnp.dot(y_ref[...], wb_s[n], preferred_element_type=F32)
        acc = gate * up if acc is None else acc + gate * up
    o_ref[...] = acc.astype(o_ref.dtype)


def _merge(h16, ya, yb, yc, w_gate, b_gate, w_branch, l, tn=256, tm=512):
    tp = h16.shape[0]
    nj = D_MODEL // tn
    wg_spec = lambda n: pl.BlockSpec((None, D_MODEL, tn), lambda j, i: (l, 0, n * nj + j))
    bg_spec = lambda n: pl.BlockSpec((None, 1, tn), lambda j, i: (l, 0, n * nj + j))
    y_spec = pl.BlockSpec((tm, D_HEADS), lambda j, i: (i, 0))
    return pl.pallas_call(
        _merge_kernel,
        grid=(nj, tp // tm),
        in_specs=[pl.BlockSpec((tm, D_MODEL), lambda j, i: (i, 0)), y_spec, y_spec, y_spec,
                  wg_spec(0), wg_spec(1), wg_spec(2), bg_spec(0), bg_spec(1), bg_spec(2),
                  pl.BlockSpec((None, N_BRANCH, D_HEADS, tn), lambda j, i: (l, 0, 0, j))],
        out_specs=pl.BlockSpec((tm, tn), lambda j, i: (i, j)),
        out_shape=jax.ShapeDtypeStruct((tp, D_MODEL), BF16),
        scratch_shapes=[pltpu.VMEM((N_BRANCH, D_MODEL, tn), BF16),
                        pltpu.VMEM((N_BRANCH, D_HEADS, tn), BF16)],
        compiler_params=_cparams(("arbitrary", "arbitrary"), 56),
        name="branch_merge",
    )(h16, ya, yb, yc, w_gate, w_gate, w_gate, b_gate, b_gate, b_gate, w_branch)


ROW_TILES = D_MODEL // (2 * LANES)
HIGH_HALF = -65536


def _store_rows_packed(ref, x):
    rows = x.shape[0]
    half = D_MODEL // 2
    bits = lax.bitcast_convert_type(x.astype(BF16).astype(F32), I32)
    words = lax.shift_right_logical(bits[:, :half], 16) | bits[:, half:]
    for c in range(ROW_TILES):
        ref[pl.ds(c, rows, stride=ROW_TILES), :] = words[:, c * LANES:(c + 1) * LANES]


def _load_rows_packed(ref, first_row, rows):
    words = [ref[pl.ds(first_row * ROW_TILES + c, rows, stride=ROW_TILES), :] for c in range(ROW_TILES)]
    low = [lax.bitcast_convert_type(lax.shift_left(w, 16), F32) for w in words]
    high = [lax.bitcast_convert_type(w & HIGH_HALF, F32) for w in words]
    return jnp.concatenate(low + high, axis=-1)


def _oproj_ln_kernel(m_ref, w_ref, h_ref, g_ref, b_ref, ho_ref, h16o_ref, hpk_ref):
    mix = jnp.dot(m_ref[...], w_ref[...], preferred_element_type=F32)
    y = _ln_rows(DEEPNORM_ALPHA * h_ref[...] + mix, g_ref[...], b_ref[...])
    ho_ref[...] = y
    h16o_ref[...] = y.astype(BF16)
    _store_rows_packed(hpk_ref, y)


def _oproj_ln(merged, w_o16, h, g, b, tm=256):
    tp, d = h.shape
    assert tp % tm == 0
    row = pl.BlockSpec((tm, d), lambda i: (i, 0))
    vec = pl.BlockSpec((1, d), lambda i: (0, 0))
    return pl.pallas_call(
        _oproj_ln_kernel,
        grid=(tp // tm,),
        in_specs=[row, pl.BlockSpec((d, d), lambda i: (0, 0)), row, vec, vec],
        out_specs=[row, row, pl.BlockSpec((tm * ROW_TILES, LANES), lambda i: (i, 0))],
        out_shape=[jax.ShapeDtypeStruct((tp, d), F32), jax.ShapeDtypeStruct((tp, d), BF16),
                   jax.ShapeDtypeStruct((tp * ROW_TILES, LANES), I32)],
        compiler_params=_cparams(("parallel",), 48),
        name="oproj_ln",
    )(merged, w_o16, h, g.reshape(1, d), b.reshape(1, d))


def _router_kernel(h_ref, w_ref, b_ref, idx_ref, wts_ref, rank_ref, cnt_ref, base_s):
    i = pl.program_id(0)
    tm = h_ref.shape[0]

    @pl.when(i == 0)
    def _():
        base_s[...] = jnp.zeros_like(base_s)

    xh, xl = _split_bf16(h_ref[...], 2)
    wh, wl = _split_bf16(w_ref[...], 2)
    dot = functools.partial(jnp.dot, preferred_element_type=F32)
    scores = jax.nn.sigmoid(dot(xh, wh) + dot(xh, wl) + dot(xl, wh))
    lane = lax.broadcasted_iota(I32, (tm, LANES), 1).astype(F32)
    sel = jnp.where(lane < N_EXPERTS, scores + b_ref[...], NEG_INF)
    chosen = jnp.zeros((tm, LANES), F32)
    idx_out = jnp.zeros((tm, LANES), F32)
    sc_out = jnp.zeros((tm, LANES), F32)
    picks = []
    for k in range(TOP_K):
        m = jnp.max(sel, -1, keepdims=True)
        ik = jnp.min(jnp.where(sel == m, lane, float(LANES)), -1, keepdims=True)
        hit = lane == ik
        sk = jnp.sum(jnp.where(hit, scores, 0.0), -1, keepdims=True)
        sel = jnp.where(hit, NEG_INF, sel)
        chosen = jnp.where(hit, 1.0, chosen)
        idx_out = jnp.where(lane == k, ik, idx_out)
        sc_out = jnp.where(lane == k, sk, sc_out)
        picks.append(hit)
    total = jnp.sum(sc_out, -1, keepdims=True)
    wts_ref[...] = sc_out / total * ROUTED_SCALE
    idx_ref[...] = idx_out.T[0:TOP_K, :].astype(I32)
    before = (lax.broadcasted_iota(I32, (tm, tm), 0) > lax.broadcasted_iota(I32, (tm, tm), 1)).astype(BF16)
    count_before = dot(before, chosen.astype(BF16)) + base_s[0:1, :]
    rank_out = jnp.zeros((tm, LANES), F32)
    for k in range(TOP_K):
        rk = jnp.sum(jnp.where(picks[k], count_before, 0.0), -1, keepdims=True)
        rank_out = jnp.where(lane == k, rk, rank_out)
    rank_ref[...] = rank_out.T[0:TOP_K, :].astype(I32)
    base_s[...] = base_s[...] + jnp.sum(chosen, 0, keepdims=True)
    cnt_ref[...] = base_s[...].astype(I32)


def _router(h, router_w, router_bias, tm=ROW_BLOCK):
    tp, d = h.shape
    w_pad = jnp.zeros((d, LANES), F32).at[:, :N_EXPERTS].set(router_w)
    b_pad = jnp.zeros((1, LANES), F32).at[0, :N_EXPERTS].set(router_bias)
    tok = pl.BlockSpec((tm, LANES), lambda i: (i, 0))
    per_k = pl.BlockSpec((TOP_K, tm), lambda i: (0, i))
    return pl.pallas_call(
        _router_kernel,
        grid=(tp // tm,),
        in_specs=[pl.BlockSpec((tm, d), lambda i: (i, 0)),
                  pl.BlockSpec((d, LANES), lambda i: (0, 0)),
                  pl.BlockSpec((1, LANES), lambda i: (0, 0))],
        out_specs=[per_k, tok, per_k, pl.BlockSpec((8, LANES), lambda i: (0, 0))],
        out_shape=[jax.ShapeDtypeStruct((TOP_K, tp), I32), jax.ShapeDtypeStruct((tp, LANES), F32),
                   jax.ShapeDtypeStruct((TOP_K, tp), I32), jax.ShapeDtypeStruct((8, LANES), I32)],
        scratch_shapes=[pltpu.VMEM((8, LANES), F32)],
        compiler_params=_cparams(("arbitrary",)),
        name="moe_router",
    )(h, w_pad, b_pad)


MOE_ROWS = 256
DMA_UNROLL = 8


def _row_span(r):
    return pl.ds(pl.multiple_of(r * ROW_TILES, ROW_TILES), ROW_TILES)


def _ffn_kernel(be_ref, nu_ref, tok0_ref, tokn_ref, dstp_ref, dstl_ref, h_hbm, wg_ref, wu_ref, wd_ref, y_hbm,
                x0, x1, y0, y1, gsem, ssem, wg_s, wu_s, wd_s):
    i = pl.program_id(0)
    n_used = nu_ref[0]
    even = i % 2 == 0
    whole = pl.ds(0, MOE_ROWS * ROW_TILES)

    def gather_row(tok_ref, r, x_next, sem):
        pltpu.make_async_copy(h_hbm.at[_row_span(tok_ref[0, 0, r]), :], x_next.at[_row_span(r), :], sem).start()

    def scatter_row(dst_ref, r, y_from, sem):
        pltpu.make_async_copy(y_from.at[_row_span(r), :], y_hbm.at[_row_span(dst_ref[0, 0, r]), :], sem).start()

    def rolled(fn):
        def body(g, carry):
            for u in range(DMA_UNROLL):
                fn(g * DMA_UNROLL + u)
            return carry

        lax.fori_loop(0, MOE_ROWS // DMA_UNROLL, body, 0)

    def gather_wait(x_buf, sem):
        pltpu.make_async_copy(h_hbm.at[whole, :], x_buf, sem).wait()

    def scatter_wait(y_buf, sem):
        pltpu.make_async_copy(y_buf, y_hbm.at[whole, :], sem).wait()

    @pl.when(i == 0)
    def _():
        y1[...] = jnp.zeros_like(y1)
        rolled(lambda r: gather_row(tok0_ref, r, x0, gsem.at[0]))

    prev = be_ref[jnp.maximum(i - 1, 0)]

    @pl.when((i == 0) | (be_ref[i] != prev))
    def _():
        wg_s[...] = wg_ref[...].astype(BF16)
        wu_s[...] = wu_ref[...].astype(BF16)
        wd_s[...] = wd_ref[...].astype(BF16)

    def step(x_cur, x_next, y_cur, y_prev, cur, nxt):
        gather_wait(x_cur, gsem.at[cur])

        @pl.when(i >= 1)
        def _():
            scatter_wait(y_cur, ssem.at[cur])

        for r in range(MOE_ROWS):
            scatter_row(dstp_ref, r, y_prev, ssem.at[nxt])
        for r in range(MOE_ROWS):
            gather_row(tokn_ref, r, x_next, gsem.at[nxt])
        x = _load_rows_packed(x_cur, 0, MOE_ROWS).astype(BF16)
        hid = _silu(jnp.dot(x, wg_s[...], preferred_element_type=F32)) * jnp.dot(
            x, wu_s[...], preferred_element_type=F32)
        _store_rows_packed(y_cur, jnp.dot(hid.astype(BF16), wd_s[...], preferred_element_type=F32))

        @pl.when(i == n_used - 1)
        def _():
            gather_wait(x_next, gsem.at[nxt])
            scatter_wait(y_prev, ssem.at[nxt])
            rolled(lambda r: scatter_row(dstl_ref, r, y_cur, ssem.at[cur]))
            scatter_wait(y_cur, ssem.at[cur])

    @pl.when(even & (i < n_used))
    def _():
        step(x0, x1, y0, y1, 0, 1)

    @pl.when(jnp.logical_not(even) & (i < n_used))
    def _():
        step(x1, x0, y1, y0, 1, 0)


def _expert_ffn(h_pk, row_tok, row_dst, block_e, n_used, e_gate, e_up, e_down, l):
    d = D_MODEL
    tokens = h_pk.shape[0] // ROW_TILES
    nb = block_e.shape[0]
    tok3 = row_tok.reshape(nb, 1, MOE_ROWS)
    spill = TOP_K * tokens + jnp.arange(MOE_ROWS, dtype=I32)
    dst3 = jnp.concatenate([spill, row_dst]).reshape(nb + 1, 1, MOE_ROWS)
    smem_blk = lambda f: pl.BlockSpec((1, 1, MOE_ROWS), f, memory_space=pltpu.SMEM)
    buf = pltpu.VMEM((MOE_ROWS * ROW_TILES, LANES), I32)
    grid_spec = pltpu.PrefetchScalarGridSpec(
        num_scalar_prefetch=2,
        grid=(nb,),
        in_specs=[smem_blk(lambda i, be, nu: (0, 0, 0)),
                  smem_blk(lambda i, be, nu: (jnp.minimum(i + 1, nb - 1), 0, 0)),
                  smem_blk(lambda i, be, nu: (i, 0, 0)),
                  smem_blk(lambda i, be, nu: (nu[0], 0, 0)),
                  pl.BlockSpec(memory_space=pl.ANY),
                  pl.BlockSpec((None, None, d, D_EXPERT), lambda i, be, nu: (l, be[i], 0, 0)),
                  pl.BlockSpec((None, None, d, D_EXPERT), lambda i, be, nu: (l, be[i], 0, 0)),
                  pl.BlockSpec((None, None, D_EXPERT, d), lambda i, be, nu: (l, be[i], 0, 0))],
        out_specs=pl.BlockSpec(memory_space=pl.ANY),
        scratch_shapes=[buf, buf, buf, buf,
                        pltpu.SemaphoreType.DMA((2,)),
                        pltpu.SemaphoreType.DMA((2,)),
                        pltpu.VMEM((d, D_EXPERT), BF16),
                        pltpu.VMEM((d, D_EXPERT), BF16),
                        pltpu.VMEM((D_EXPERT, d), BF16)],
    )
    return pl.pallas_call(
        _ffn_kernel,
        grid_spec=grid_spec,
        out_shape=jax.ShapeDtypeStruct(((TOP_K * tokens + MOE_ROWS) * ROW_TILES, LANES), I32),
        compiler_params=_cparams(("arbitrary",), 52),
        name="expert_ffn",
    )(block_e, n_used, tok3, tok3, dst3, dst3, h_pk, e_gate, e_up, e_down)


def _combine_kernel(*refs, final):
    y_refs = refs[:TOP_K]
    h_ref, h16_ref, wts_ref, sg_ref, su_ref, sd_ref, g_ref, b_ref = refs[TOP_K:TOP_K + 8]
    out_refs = refs[TOP_K + 8:]
    tm = h_ref.shape[0]
    x = h16_ref[...]
    hid = _silu(jnp.dot(x, sg_ref[...], preferred_element_type=F32)) * jnp.dot(
        x, su_ref[...], preferred_element_type=F32)
    acc = jnp.dot(hid.astype(BF16), sd_ref[...], preferred_element_type=F32)
    wts = wts_ref[...]
    for k in range(TOP_K):
        acc = acc + _load_rows_packed(y_refs[k], 0, tm) * wts[:, k:k + 1]
    y = _ln_rows(DEEPNORM_ALPHA * h_ref[...] + acc, g_ref[...], b_ref[...])
    out_refs[0][...] = y
    if not final:
        out_refs[1][...] = y.astype(BF16)


def _combine(y_lin, h, h16, wts, sg16, su16, sd16, g, b, lp, final):
    tp, d = h.shape
    tm = ROW_BLOCK
    nt = tp // tm
    bps = lp // tm
    row = pl.BlockSpec((tm, d), lambda i: (i, 0))
    vec = pl.BlockSpec((1, d), lambda i: (0, 0))
    y_specs = [pl.BlockSpec((tm * ROW_TILES, LANES), lambda i, k=k: (k * nt + i, 0)) for k in range(TOP_K)]
    if final:
        out_specs = [pl.BlockSpec((tm, d), lambda i: (_frame_block(i, bps), 0))]
        out_shape = [jax.ShapeDtypeStruct((tp // lp * (lp - tm), d), F32)]
    else:
        out_specs = [row, row]
        out_shape = [jax.ShapeDtypeStruct((tp, d), F32), jax.ShapeDtypeStruct((tp, d), BF16)]
    return pl.pallas_call(
        functools.partial(_combine_kernel, final=final),
        grid=(nt,),
        in_specs=y_specs + [row, row,
                            pl.BlockSpec((tm, LANES), lambda i: (i, 0)),
                            pl.BlockSpec((d, D_EXPERT), lambda i: (0, 0)),
                            pl.BlockSpec((d, D_EXPERT), lambda i: (0, 0)),
                            pl.BlockSpec((D_EXPERT, d), lambda i: (0, 0)),
                            vec, vec],
        out_specs=out_specs,
        out_shape=out_shape,
        compiler_params=_cparams(("arbitrary",), 56),
        name="moe_combine",
    )(*([y_lin] * TOP_K), h, h16, wts, sg16, su16, sd16, g.reshape(1, d), b.reshape(1, d))


def _routing_tables(idx, rank, counts):
    tp = idx.shape[1]
    n_blocks = -(-(tp * TOP_K) // MOE_ROWS) + N_EXPERTS
    n_rows = n_blocks * MOE_ROWS
    padded = (counts + MOE_ROWS - 1) // MOE_ROWS * MOE_ROWS
    pad_end = jnp.cumsum(padded)
    pad_start = pad_end - padded
    experts = jnp.arange(N_EXPERTS, dtype=I32)[:, None, None]
    first_row = jnp.sum(jnp.where(idx[None] == experts, pad_start[:, None, None], 0), axis=0)
    dest = first_row + rank
    info = jnp.arange(tp, dtype=I32)[None, :] * TOP_K + jnp.arange(TOP_K, dtype=I32)[:, None] + 1
    row_info = jnp.zeros((n_rows,), I32).at[dest.reshape(-1)].set(info.reshape(-1), unique_indices=True)
    filled = row_info > 0
    tok = (row_info - 1) >> 3
    k = (row_info - 1) & (TOP_K - 1)
    row_tok = jnp.where(filled, tok, 0)
    row_dst = jnp.where(filled, k * tp + tok, TOP_K * tp + jnp.arange(n_rows, dtype=I32) % MOE_ROWS)
    block_start = jnp.arange(n_blocks, dtype=I32) * MOE_ROWS
    block_e = jnp.minimum(jnp.sum((pad_end[None, :] <= block_start[:, None]).astype(I32), axis=1), N_EXPERTS - 1)
    n_used = (pad_end[-1] // MOE_ROWS).astype(I32).reshape(1)
    return row_tok, row_dst, block_e, n_used


def kernel(x, meta, ln_in_g, ln_in_b, w_in, pool_w, pool_scale, conv_w, A_log, dt_bias, gdn_norm_g, w_branch, w_gate, b_gate, w_o, ln1_g, ln1_b, router_w, router_bias, exp_w_gate, exp_w_up, exp_w_down, sh_w_gate, sh_w_up, sh_w_down, ln2_g, ln2_b):
    batch, seq, d = x.shape
    depth = w_in.shape[0]
    lp = PAD_ROWS + N_META + seq
    assert seq % ROW_BLOCK == 0 and d == D_MODEL
    tp = batch * lp
    h, h16 = _ln_in(x.reshape(batch * seq, d), meta, ln_in_g, ln_in_b, lp)
    b_gate3 = b_gate.reshape(depth, 1, N_BRANCH * d)
    tm = next(t for t in (1088, 512, ROW_BLOCK) if tp % t == 0)

    w_side = jnp.concatenate([w_in[:, :, N_MAIN + N_GATES:], w_in[:, :, N_MAIN:N_MAIN + N_GATES],
                              jnp.zeros((depth, d, LANES - N_GATES), F32)], axis=2)

    for l in range(depth):
        proj_main = _matmul(h16, w_in, pl.BlockSpec((None, d, 512), lambda j, i: (l, 0, j)),
                            N_MAIN, 512, tm, name="proj_main")
        proj_side = _matmul(h16, w_side, pl.BlockSpec((None, d, 640), lambda j, i: (l, 0, j)),
                            N_SIDE, 640, tm, name="proj_side")
        y_a = _pool_mixer(proj_main, pool_w[l], pool_scale[l], batch, lp)
        qkv = _gdn_conv(proj_main, conv_w[l], batch, lp)
        y_b = _gdn(qkv, proj_main, proj_side, A_log[l], dt_bias[l], gdn_norm_g[l], batch, lp)
        y_c = _stick_breaking(proj_side, batch, lp)
        merged = _merge(h16, y_a, y_b, y_c, w_gate, b_gate3, w_branch, l, tm=tm)
        h, h16, h_pk = _oproj_ln(merged, w_o[l].astype(BF16), h, ln1_g[l], ln1_b[l],
                                 tm=256 if tp % 256 == 0 else ROW_BLOCK)

        idx, wts, rank, cnt = _router(h, router_w[l], router_bias[l])
        row_tok, row_dst, block_e, n_used = _routing_tables(idx, rank, cnt[0, :N_EXPERTS])
        y_pk = _expert_ffn(h_pk, row_tok, row_dst, block_e, n_used, exp_w_gate, exp_w_up, exp_w_down, l)
        out = _combine(y_pk, h, h16, wts, sh_w_gate[l].astype(BF16), sh_w_up[l].astype(BF16),
                       sh_w_down[l].astype(BF16), ln2_g[l], ln2_b[l], lp, final=(l == depth - 1))
        if l < depth - 1:
            h, h16 = out

    return out[0].reshape(batch, seq, d)
```
